```python
import math
import jax, jax.numpy as jnp
from jax import lax
import numpy as np

D_MODEL = 1024
BATCH = 8
SEQ = 2048
DEPTH = 1
DEC_BATCH = 128
DEC_SEQ = 4
PAST_LEN = 8192
PAGE_SIZE = 128

HEAD_DIM = 64
FOX_HEADS = 8
FOX_KV_HEADS = 4
FOX_GROUP = FOX_HEADS // FOX_KV_HEADS
DIFF_HEADS = 4
DIFF_KV_HEADS = 2
DIFF_GROUP = DIFF_HEADS // DIFF_KV_HEADS
FOX_WIDTH = FOX_HEADS * HEAD_DIM
DIFF_WIDTH = DIFF_HEADS * 2 * HEAD_DIM
MIX_WIDTH = FOX_WIDTH + DIFF_WIDTH
ROT_DIM = HEAD_DIM // 4
ROPE_THETA = 500000.0
D_FF = ((8 * D_MODEL // 3 + 127) // 128) * 128
CONV_W = 3
N_META = 16
Q_BLOCK = 128
EPS = 1e-6
NEG = -1e30
FORGET_BIAS = 4.0
SPLITS = (FOX_WIDTH, FOX_KV_HEADS * HEAD_DIM, FOX_KV_HEADS * HEAD_DIM, FOX_HEADS,
          DIFF_HEADS * 2 * HEAD_DIM, DIFF_KV_HEADS * 2 * HEAD_DIM, DIFF_KV_HEADS * 2 * HEAD_DIM)
IN_WIDTH = sum(SPLITS)

kernel_name = "hymba_fox_diffattn_convffn_step"

F32 = jnp.float32


def rmsnorm(x, g):
    xf = x.astype(F32)
    y = xf * lax.rsqrt(jnp.mean(xf * xf, axis=-1, keepdims=True) + EPS)
    return (y * g.astype(F32)).astype(x.dtype)


def rotary(x, pos):
    inv_freq = ROPE_THETA ** (-jnp.arange(0, ROT_DIM, 2, dtype=F32) / ROT_DIM)
    ang = pos.astype(F32)[:, None] * inv_freq[None, :]
    shp = (1, pos.shape[0]) + (1,) * (x.ndim - 3) + (ROT_DIM // 2,)
    cos, sin = jnp.cos(ang).reshape(shp), jnp.sin(ang).reshape(shp)
    xr = x[..., :ROT_DIM].astype(F32)
    x1, x2 = xr[..., :ROT_DIM // 2], xr[..., ROT_DIM // 2:]
    rot = jnp.concatenate([x1 * cos - x2 * sin, x2 * cos + x1 * sin], axis=-1)
    return jnp.concatenate([rot.astype(x.dtype), x[..., ROT_DIM:]], axis=-1)


def mixer_inputs(xn, pos, w_in, b_f, fox_qn, fox_kn, diff_qn, diff_kn):
    B, L, _ = xn.shape
    idx = np.cumsum(SPLITS)[:-1].tolist()
    fq, fk, fv, ff, dq, dk, dv = jnp.split(xn @ w_in, idx, axis=-1)
    fq = rmsnorm(fq.reshape(B, L, FOX_KV_HEADS, FOX_GROUP, HEAD_DIM), fox_qn)
    fk = rmsnorm(fk.reshape(B, L, FOX_KV_HEADS, HEAD_DIM), fox_kn)
    fv = fv.reshape(B, L, FOX_KV_HEADS, HEAD_DIM)
    logf = jax.nn.log_sigmoid((ff + b_f).astype(F32))
    dq = rotary(rmsnorm(dq.reshape(B, L, DIFF_KV_HEADS, DIFF_GROUP, 2, HEAD_DIM), diff_qn), pos)
    dk = rotary(rmsnorm(dk.reshape(B, L, DIFF_KV_HEADS, 2, HEAD_DIM), diff_kn), pos)
    dv = dv.reshape(B, L, DIFF_KV_HEADS, 2 * HEAD_DIM)
    return fq, fk, fv, logf, dq, dk, dv


def over_query_blocks(fn, *qs):
    B, L = qs[0].shape[:2]
    meta = fn(*(a[:, :N_META] for a in qs), jnp.arange(N_META))
    nb = (L - N_META) // Q_BLOCK

    def to_blocks(a):
        r = a[:, N_META:]
        return jnp.moveaxis(r.reshape((B, nb, Q_BLOCK) + r.shape[2:]), 1, 0)

    pos = N_META + jnp.arange(nb * Q_BLOCK).reshape(nb, Q_BLOCK)
    out = lax.map(lambda a: fn(*a[:-1], a[-1]), tuple(to_blocks(a) for a in qs) + (pos,))
    out = jnp.moveaxis(out, 0, 1).reshape((B, nb * Q_BLOCK) + out.shape[3:])
    return jnp.concatenate([meta, out], axis=1)


def fox_prompt(fq, fk, fv, logf):
    B, L = fk.shape[:2]
    C = jnp.cumsum(logf, axis=1).reshape(B, L, FOX_KV_HEADS, FOX_GROUP)
    ck = C.transpose(0, 2, 3, 1)
    kf, vf = fk.astype(F32), fv.astype(F32)
    kpos = jnp.arange(L)
    scale = HEAD_DIM ** -0.5

    def block(q_blk, cq_blk, qpos):
        s = jnp.einsum('bqhgd,bkhd->bhgqk', q_blk.astype(F32), kf) * scale
        s = s + cq_blk.transpose(0, 2, 3, 1)[..., None] - ck[..., None, :]
        s = jnp.where(kpos[None, :] <= qpos[:, None], s, NEG)
        p = jax.nn.softmax(s, axis=-1)
        o = jnp.einsum('bhgqk,bkhd->bqhgd', p, vf)
        return o.reshape(B, q_blk.shape[1], FOX_HEADS, HEAD_DIM).astype(fk.dtype)

    return over_query_blocks(block, fq, C)


def diff_prompt(dq, dk, dv, lam):
    B, L = dk.shape[:2]
    k1, k2 = dk[..., 0, :].astype(F32), dk[..., 1, :].astype(F32)
    vf = dv.astype(F32)
    kpos = jnp.arange(L)
    scale = HEAD_DIM ** -0.5

    def block(q_blk, qpos):
        qf = q_blk.astype(F32) * scale
        mask = kpos[None, :] <= qpos[:, None]
        s1 = jnp.where(mask, jnp.einsum('bqhgd,bkhd->bhgqk', qf[..., 0, :], k1), NEG)
        s2 = jnp.where(mask, jnp.einsum('bqhgd,bkhd->bhgqk', qf[..., 1, :], k2), NEG)
        p = jax.nn.softmax(s1, axis=-1) - lam * jax.nn.softmax(s2, axis=-1)
        o = jnp.einsum('bhgqk,bkhd->bqhgd', p, vf)
        return o.reshape(B, q_blk.shape[1], DIFF_HEADS, 2 * HEAD_DIM).astype(dk.dtype)

    return over_query_blocks(block, dq)


def online_update(m, lsum, acc, s, v):
    m_new = jnp.maximum(m, s.max(axis=-1))
    corr = jnp.exp(m - m_new)
    p = jnp.exp(s - m_new[..., None])
    lsum = lsum * corr + p.sum(axis=-1)
    acc = acc * corr[..., None] + jnp.einsum('bhgqk,bkhd->bhgqd', p, v)
    return m_new, lsum, acc


def fox_sample(fq, fk, fv, logf, cache_k, cache_v, cache_logf, page_table, layer):
    DB, T = fk.shape[:2]
    NP = page_table.shape[1]
    scale = HEAD_DIM ** -0.5
    lp = cache_logf[layer, page_table].reshape(DB, NP * PAGE_SIZE, FOX_HEADS).astype(F32)
    R = lax.cumsum(lp, axis=1, reverse=True) - lp
    R = R.reshape(DB, NP, PAGE_SIZE, FOX_KV_HEADS, FOX_GROUP).transpose(1, 0, 3, 4, 2)
    Qc = jnp.cumsum(logf, axis=1).reshape(DB, T, FOX_KV_HEADS, FOX_GROUP).transpose(0, 2, 3, 1)
    qf = fq.astype(F32) * scale

    def step(carry, xs):
        m, lsum, acc = carry
        pid, r = xs
        k = cache_k[layer, pid].astype(F32)
        v = cache_v[layer, pid].astype(F32)
        s = jnp.einsum('bqhgd,bkhd->bhgqk', qf, k) + Qc[..., None] + r[..., None, :]
        return online_update(m, lsum, acc, s, v), None

    init = (jnp.full((DB, FOX_KV_HEADS, FOX_GROUP, T), NEG, F32),
            jnp.zeros((DB, FOX_KV_HEADS, FOX_GROUP, T), F32),
            jnp.zeros((DB, FOX_KV_HEADS, FOX_GROUP, T, HEAD_DIM), F32))
    (m, lsum, acc), _ = lax.scan(step, init, (page_table.T, R))
    causal = jnp.arange(T)[None, :] <= jnp.arange(T)[:, None]
    s = jnp.einsum('bqhgd,bkhd->bhgqk', qf, fk.astype(F32)) + Qc[..., None] - Qc[..., None, :]
    m, lsum, acc = online_update(m, lsum, acc, jnp.where(causal, s, NEG), fv.astype(F32))
    o = acc / lsum[..., None]
    return o.transpose(0, 3, 1, 2, 4).reshape(DB, T, FOX_HEADS, HEAD_DIM).astype(fk.dtype)


def diff_sample(dq, dk, dv, lam, cache_k, cache_v, page_table, layer):
    DB, T = dk.shape[:2]
    scale = HEAD_DIM ** -0.5
    qf = dq.astype(F32) * scale
    q1, q2 = qf[..., 0, :], qf[..., 1, :]

    def step(carry, pid):
        m1, l1, a1, m2, l2, a2 = carry
        k = cache_k[layer, pid].astype(F32)
        v = cache_v[layer, pid].astype(F32)
        m1, l1, a1 = online_update(m1, l1, a1, jnp.einsum('bqhgd,bkhd->bhgqk', q1, k[..., 0, :]), v)
        m2, l2, a2 = online_update(m2, l2, a2, jnp.einsum('bqhgd,bkhd->bhgqk', q2, k[..., 1, :]), v)
        return (m1, l1, a1, m2, l2, a2), None

    zm = jnp.full((DB, DIFF_KV_HEADS, DIFF_GROUP, T), NEG, F32)
    zl = jnp.zeros((DB, DIFF_KV_HEADS, DIFF_GROUP, T), F32)
    za = jnp.zeros((DB, DIFF_KV_HEADS, DIFF_GROUP, T, 2 * HEAD_DIM), F32)
    (m1, l1, a1, m2, l2, a2), _ = lax.scan(step, (zm, zl, za, zm, zl, za), page_table.T)
    causal = jnp.arange(T)[None, :] <= jnp.arange(T)[:, None]
    kf, vf = dk.astype(F32), dv.astype(F32)
    s1 = jnp.where(causal, jnp.einsum('bqhgd,bkhd->bhgqk', q1, kf[..., 0, :]), NEG)
    s2 = jnp.where(causal, jnp.einsum('bqhgd,bkhd->bhgqk', q2, kf[..., 1, :]), NEG)
    m1, l1, a1 = online_update(m1, l1, a1, s1, vf)
    m2, l2, a2 = online_update(m2, l2, a2, s2, vf)
    o = a1 / l1[..., None] - lam * a2 / l2[..., None]
    return o.transpose(0, 3, 1, 2, 4).reshape(DB, T, DIFF_HEADS, 2 * HEAD_DIM).astype(dk.dtype)


def diff_lambda_value(lp, lam_init):
    lpf = lp.astype(F32)
    return jnp.exp(jnp.sum(lpf[0] * lpf[1])) - jnp.exp(jnp.sum(lpf[2] * lpf[3])) + lam_init


def merge_heads(fo, do, fox_on_g, subln_g, lam_init, w_o):
    B, L = fo.shape[:2]
    fo = rmsnorm(fo, fox_on_g).reshape(B, L, FOX_WIDTH)
    do = (rmsnorm(do, subln_g) * (1.0 - lam_init)).reshape(B, L, DIFF_WIDTH)
    return jnp.concatenate([fo, do], axis=-1) @ w_o


def conv_ffn(xn, conv_state, w_up, conv_w, conv_b, w_down):
    h = xn @ w_up
    L = h.shape[1]
    hp = jnp.concatenate([conv_state.astype(h.dtype), h], axis=1)
    c = conv_b + sum(conv_w[j] * hp[:, j:j + L] for j in range(CONV_W))
    g, u = jnp.split(c, 2, axis=-1)
    return (jax.nn.silu(g) * u) @ w_down, hp[:, -(CONV_W - 1):]


def setup_inputs(seed: int = 0) -> dict:
    key = jax.random.key(seed)
    ks = jax.random.split(key, 32)
    n_pages = PAST_LEN // PAGE_SIZE
    n_used = DEC_BATCH * n_pages
    n_phys = n_used + max(1, n_used // 4)

    def nrm(k, shape, scale=1.0):
        return jax.random.normal(k, shape, F32) * scale

    page_table = jax.random.permutation(ks[0], n_phys)[:n_used].reshape(DEC_BATCH, n_pages).astype(jnp.int32)
    return {
        "x_prompt": nrm(ks[1], (BATCH, SEQ, D_MODEL)),
        "x_sample": nrm(ks[2], (DEC_BATCH, DEC_SEQ, D_MODEL)),
        "cache_fox_k": nrm(ks[3], (DEPTH, n_phys, PAGE_SIZE, FOX_KV_HEADS, HEAD_DIM)),
        "cache_fox_v": nrm(ks[4], (DEPTH, n_phys, PAGE_SIZE, FOX_KV_HEADS, HEAD_DIM)),
        "cache_fox_logf": jax.nn.log_sigmoid(FORGET_BIAS + nrm(ks[5], (DEPTH, n_phys, PAGE_SIZE, FOX_HEADS))),
        "cache_diff_k": nrm(ks[6], (DEPTH, n_phys, PAGE_SIZE, DIFF_KV_HEADS, 2, HEAD_DIM)),
        "cache_diff_v": nrm(ks[7], (DEPTH, n_phys, PAGE_SIZE, DIFF_KV_HEADS, 2 * HEAD_DIM)),
        "state_ffn_conv": nrm(ks[8], (DEPTH, DEC_BATCH, CONV_W - 1, 2 * D_FF)),
        "page_table": page_table,
        "meta_tokens": nrm(ks[9], (N_META, D_MODEL)),
        "attn_norm_g": 1.0 + nrm(ks[10], (DEPTH, D_MODEL), 0.05),
        "w_in": nrm(ks[11], (DEPTH, D_MODEL, IN_WIDTH), D_MODEL ** -0.5),
        "b_f": FORGET_BIAS + nrm(ks[12], (DEPTH, FOX_HEADS), 0.5),
        "fox_qn_g": 1.0 + nrm(ks[13], (DEPTH, HEAD_DIM), 0.05),
        "fox_kn_g": 1.0 + nrm(ks[14], (DEPTH, HEAD_DIM), 0.05),
        "fox_on_g": 1.0 + nrm(ks[15], (DEPTH, FOX_HEADS, HEAD_DIM), 0.05),
        "diff_qn_g": 1.0 + nrm(ks[16], (DEPTH, HEAD_DIM), 0.05),
        "diff_kn_g": 1.0 + nrm(ks[17], (DEPTH, HEAD_DIM), 0.05),
        "diff_lambda": nrm(ks[18], (DEPTH, 4, HEAD_DIM), 0.1),
        "diff_subln_g": 1.0 + nrm(ks[19], (DEPTH, 2 * HEAD_DIM), 0.05),
        "w_o": nrm(ks[20], (DEPTH, MIX_WIDTH, D_MODEL), MIX_WIDTH ** -0.5),
        "ffn_norm_g": 1.0 + nrm(ks[21], (DEPTH, D_MODEL), 0.05),
        "w_up": nrm(ks[22], (DEPTH, D_MODEL, 2 * D_FF), D_MODEL ** -0.5),
        "conv_w": nrm(ks[23], (DEPTH, CONV_W, 2 * D_FF), CONV_W ** -0.5),
        "conv_b": nrm(ks[24], (DEPTH, 2 * D_FF), 0.01),
        "w_down": nrm(ks[25], (DEPTH, D_FF, D_MODEL), D_FF ** -0.5),
    }


def reference(x_prompt, x_sample, cache_fox_k, cache_fox_v, cache_fox_logf, cache_diff_k, cache_diff_v,
              state_ffn_conv, page_table, meta_tokens, attn_norm_g, w_in, b_f, fox_qn_g, fox_kn_g, fox_on_g,
              diff_qn_g, diff_kn_g, diff_lambda, diff_subln_g, w_o, ffn_norm_g, w_up, conv_w, conv_b, w_down):
    B = x_prompt.shape[0]
    DB, T = x_sample.shape[:2]
    xp = jnp.concatenate([jnp.broadcast_to(meta_tokens[None].astype(x_prompt.dtype), (B, N_META, D_MODEL)),
                          x_prompt], axis=1)
    L = xp.shape[1]
    pos_p = jnp.arange(L)
    pos_s = PAST_LEN + jnp.arange(T)
    xs = x_sample
    pk, pv, pf, pdk, pdv, pc = [], [], [], [], [], []
    sk, sv, sf, sdk, sdv, sc = [], [], [], [], [], []
    for layer in range(DEPTH):
        lam_init = 0.8 - 0.6 * math.exp(-0.3 * layer)
        lam = diff_lambda_value(diff_lambda[layer], lam_init)
        proj_w = (w_in[layer], b_f[layer], fox_qn_g[layer], fox_kn_g[layer], diff_qn_g[layer], diff_kn_g[layer])
        fq, fk, fv, logf, dq, dk, dv = mixer_inputs(rmsnorm(xp, attn_norm_g[layer]), pos_p, *proj_w)
        fo = fox_prompt(fq, fk, fv, logf)
        do = diff_prompt(dq, dk, dv, lam)
        xp = xp + merge_heads(fo, do, fox_on_g[layer], diff_subln_g[layer], lam_init, w_o[layer])
        y, cst = conv_ffn(rmsnorm(xp, ffn_norm_g[layer]), jnp.zeros((B, CONV_W - 1, 2 * D_FF), xp.dtype),
                          w_up[layer], conv_w[layer], conv_b[layer], w_down[layer])
        xp = xp + y
        pk.append(fk); pv.append(fv); pf.append(logf); pdk.append(dk); pdv.append(dv); pc.append(cst)
        fq, fk, fv, logf, dq, dk, dv = mixer_inputs(rmsnorm(xs, attn_norm_g[layer]), pos_s, *proj_w)
        fo = fox_sample(fq, fk, fv, logf, cache_fox_k, cache_fox_v, cache_fox_logf, page_table, layer)
        do = diff_sample(dq, dk, dv, lam, cache_diff_k, cache_diff_v, page_table, layer)
        xs = xs + merge_heads(fo, do, fox_on_g[layer], diff_subln_g[layer], lam_init, w_o[layer])
        y, cst = conv_ffn(rmsnorm(xs, ffn_norm_g[layer]), state_ffn_conv[layer],
                          w_up[layer], conv_w[layer], conv_b[layer], w_down[layer])
        xs = xs + y
        sk.append(fk); sv.append(fv); sf.append(logf); sdk.append(dk); sdv.append(dv); sc.append(cst)
    y_prompt = xp[:, N_META:]
    y_sample = xs
    return (y_prompt, y_sample,
            jnp.stack(pk), jnp.stack(pv), jnp.stack(pf), jnp.stack(pdk), jnp.stack(pdv), jnp.stack(pc),
            jnp.stack(sk), jnp.stack(sv), jnp.stack(sf), jnp.stack(sdk), jnp.stack(sdv), jnp.stack(sc))
```

```python
import functools
import math

import numpy as np
import jax
import jax.numpy as jnp
from jax import lax
from jax.experimental import pallas as pl
from jax.experimental.pallas import tpu as pltpu

F32 = jnp.float32
BF16 = jnp.bfloat16

HEAD_DIM = 64
FOX_HEADS = 8
FOX_KV_HEADS = 4
DIFF_HEADS = 4
DIFF_KV_HEADS = 2
FOX_WIDTH = FOX_HEADS * HEAD_DIM
FOX_KV_WIDTH = FOX_KV_HEADS * HEAD_DIM
DIFF_WIDTH = DIFF_HEADS * 2 * HEAD_DIM
DIFF_KV_WIDTH = DIFF_KV_HEADS * 2 * HEAD_DIM
ROT_DIM = HEAD_DIM // 4
ROPE_THETA = 500000.0
CONV_W = 3
N_META = 16
EPS = 1e-6
NEG = -1e30
SCALE = HEAD_DIM ** -0.5

LANES = 128
MXU_DIM = 256
ROW_TILE = 256
FF_CHUNK = 256
PAGES_PER_STEP = 8
VMEM_LIMIT = 56 * 1024 * 1024

_FQ, _FK, _FV, _DQ, _DK, _DV, _FF, _W_TOTAL = 0, 512, 768, 1024, 1536, 1792, 2048, 2176


def _split3(x):
    hi = x.astype(BF16)
    r = x - hi.astype(F32)
    mid = r.astype(BF16)
    lo = (r - mid.astype(F32)).astype(BF16)
    return hi, mid, lo


def _dot(a, b):
    return jnp.dot(a, b, preferred_element_type=F32)


def _dot_nt(a, b):
    return lax.dot_general(a, b, (((1,), (1,)), ((), ())), preferred_element_type=F32)


def _group_mean_sq(x, gmat):
    outs = []
    for c in range(x.shape[1] // MXU_DIM):
        s = x[:, c * MXU_DIM:(c + 1) * MXU_DIM]
        outs.append(_dot((s * s).astype(BF16), gmat))
    return outs[0] if len(outs) == 1 else jnp.concatenate(outs, axis=1)


def _group_rmsnorm(x, gmat, gain):
    return x * lax.rsqrt(_group_mean_sq(x, gmat) + EPS) * gain


def _proj_kernel(x_ref, gattn_ref, w_ref, gm_ref, gh_ref, bf_ref, cos_ref, sa_ref, sb_ref, tri_ref,
                 qf_ref, fk_ref, fv_ref, lf_ref, dk_ref, dv_ref,
                 fkb_ref, fvb_ref, dqb_ref, dkb_ref, dvb_ref, ccol_ref, crow_ref, carry_ref):
    i = pl.program_id(1)
    x = x_ref[0]
    xn = x * lax.rsqrt(jnp.mean(x * x, axis=-1, keepdims=True) + EPS) * gattn_ref[...]
    y = _dot(xn.astype(BF16), w_ref[...])
    gm = gm_ref[...]
    cos_t, sin_a, sin_b = cos_ref[...], sa_ref[...], sb_ref[...]

    def rope(seg):
        outs = []
        for c in range(seg.shape[1] // LANES):
            s = seg[:, c * LANES:(c + 1) * LANES]
            outs.append(s * cos_t + pltpu.roll(s, LANES - ROT_DIM // 2, 1) * sin_a
                        + pltpu.roll(s, ROT_DIM // 2, 1) * sin_b)
        return jnp.concatenate(outs, axis=1)

    fq = _group_rmsnorm(y[:, _FQ:_FK], gm, gh_ref[:, 0:512])
    qf_ref[0] = (fq * SCALE).astype(BF16)
    fk = _group_rmsnorm(y[:, _FK:_FV], gm, gh_ref[:, 512:768])
    fk_ref[0] = fk
    fkb_ref[0] = fk.astype(BF16)
    fv = y[:, _FV:_DQ]
    fv_ref[0] = fv
    fvb_ref[0] = fv.astype(BF16)
    dq = rope(_group_rmsnorm(y[:, _DQ:_DK], gm, gh_ref[:, 768:1280]))
    dqb_ref[0] = (dq * SCALE).astype(BF16)
    dk = rope(_group_rmsnorm(y[:, _DK:_DV], gm, gh_ref[:, 1280:1536]))
    dk_ref[0] = dk
    dkb_ref[0] = dk.astype(BF16)
    dv = y[:, _DV:_FF]
    dv_ref[0] = dv
    dvb_ref[0] = dv.astype(BF16)

    z = y[:, _FF:_W_TOTAL] + bf_ref[...]
    lf = jnp.minimum(z, 0.0) - jnp.log1p(jnp.exp(-jnp.abs(z)))
    lf_ref[0] = lf[:, :FOX_HEADS]

    lane = lax.broadcasted_iota(jnp.int32, lf.shape, 1)
    lf = jnp.where(lane < FOX_HEADS, lf, 0.0)
    hi, mid, lo = _split3(lf)
    tri = tri_ref[...]
    cs = _dot(tri, hi) + _dot(tri, mid) + _dot(tri, lo)

    @pl.when(i == 0)
    def _():
        carry_ref[...] = jnp.zeros_like(carry_ref)

    c = cs + carry_ref[0:1, :]
    tm = c.shape[0]
    carry_ref[0:1, :] = c[tm - 1:tm, :]
    ccol_ref[0] = c[:, :FOX_HEADS]
    crow_ref[0] = c.T[:FOX_HEADS, :]


def _rope_tables(pos):
    inv_freq = ROPE_THETA ** (-jnp.arange(0, ROT_DIM, 2, dtype=F32) / ROT_DIM)
    ang = pos.astype(F32)[:, None] * inv_freq[None, :]
    cos, sin = jnp.cos(ang), jnp.sin(ang)
    half = ROT_DIM // 2
    ones = jnp.ones((pos.shape[0], HEAD_DIM - ROT_DIM), F32)
    zeros_h = jnp.zeros((pos.shape[0], half), F32)
    zeros_r = jnp.zeros((pos.shape[0], HEAD_DIM - ROT_DIM), F32)
    cos_h = jnp.concatenate([cos, cos, ones], axis=1)
    sa_h = jnp.concatenate([-sin, zeros_h, zeros_r], axis=1)
    sb_h = jnp.concatenate([zeros_h, sin, zeros_r], axis=1)
    rep = LANES // HEAD_DIM
    return jnp.tile(cos_h, (1, rep)), jnp.tile(sa_h, (1, rep)), jnp.tile(sb_h, (1, rep))


def _block_diag_mean(group):
    idx = np.arange(MXU_DIM) // group
    return jnp.asarray((idx[:, None] == idx[None, :]).astype(np.float32) / group, dtype=BF16)


def _project(x, pos, tm, p):
    nb, lt, d = x.shape
    nt = lt // tm
    cos_t, sin_a, sin_b = _rope_tables(pos)
    tri = jnp.asarray(np.tril(np.ones((tm, tm), np.float32)), dtype=BF16)
    row = lambda w: pl.BlockSpec((1, tm, w), lambda b, i: (b, i, 0))
    const = lambda shape: pl.BlockSpec(shape, lambda b, i: (0,) * len(shape))
    tab = pl.BlockSpec((tm, LANES), lambda b, i: (i, 0))
    out_shapes = [
        jax.ShapeDtypeStruct((nb, lt, FOX_WIDTH), BF16),
        jax.ShapeDtypeStruct((nb, lt, FOX_KV_WIDTH), F32),
        jax.ShapeDtypeStruct((nb, lt, FOX_KV_WIDTH), F32),
        jax.ShapeDtypeStruct((nb, lt, FOX_HEADS), F32),
        jax.ShapeDtypeStruct((nb, lt, DIFF_KV_WIDTH), F32),
        jax.ShapeDtypeStruct((nb, lt, DIFF_KV_WIDTH), F32),
        jax.ShapeDtypeStruct((nb, lt, FOX_KV_WIDTH), BF16),
        jax.ShapeDtypeStruct((nb, lt, FOX_KV_WIDTH), BF16),
        jax.ShapeDtypeStruct((nb, lt, DIFF_WIDTH), BF16),
        jax.ShapeDtypeStruct((nb, lt, DIFF_KV_WIDTH), BF16),
        jax.ShapeDtypeStruct((nb, lt, DIFF_KV_WIDTH), BF16),
        jax.ShapeDtypeStruct((nb, lt, FOX_HEADS), F32),
        jax.ShapeDtypeStruct((nb, FOX_HEADS, lt), F32),
    ]
    out_specs = [row(FOX_WIDTH), row(FOX_KV_WIDTH), row(FOX_KV_WIDTH), row(FOX_HEADS),
                 row(DIFF_KV_WIDTH), row(DIFF_KV_WIDTH), row(FOX_KV_WIDTH), row(FOX_KV_WIDTH),
                 row(DIFF_WIDTH), row(DIFF_KV_WIDTH), row(DIFF_KV_WIDTH), row(FOX_HEADS),
                 pl.BlockSpec((1, FOX_HEADS, tm), lambda b, i: (b, 0, i))]
    return pl.pallas_call(
        _proj_kernel,
        out_shape=out_shapes,
        grid=(nb, nt),
        in_specs=[row(d), const((1, d)), const((d, _W_TOTAL)), const((MXU_DIM, MXU_DIM)),
                  const((1, 1536)), const((1, LANES)), tab, tab, tab, const((tm, tm))],
        out_specs=out_specs,
        scratch_shapes=[pltpu.VMEM((8, LANES), F32)],
        compiler_params=pltpu.CompilerParams(dimension_semantics=("parallel", "arbitrary"),
                                             vmem_limit_bytes=VMEM_LIMIT),
        name="proj",
    )(x, p["g_attn"], p["w_in"], p["gm64"], p["g_heads"], p["b_f"], cos_t, sin_a, sin_b, tri)


def _softmax_step(s, v, m_ref, l_ref, acc_ref):
    m_prev = m_ref[...]
    m_new = jnp.maximum(m_prev, jnp.max(s, axis=-1, keepdims=True))
    alpha = jnp.exp(m_prev - m_new)
    p = jnp.exp(s - m_new)
    l_ref[...] = alpha * l_ref[...] + jnp.sum(p, axis=-1, keepdims=True)
    acc_ref[...] = alpha * acc_ref[...] + _dot(p.astype(BF16), v)
    m_ref[...] = m_new


def _reset(m_ref, l_ref, acc_ref):
    m_ref[...] = jnp.full(m_ref.shape, NEG, F32)
    l_ref[...] = jnp.zeros(l_ref.shape, F32)
    acc_ref[...] = jnp.zeros(acc_ref.shape, F32)


def _fox_prompt_kernel(q_ref, k_ref, v_ref, cc_ref, cr_ref, o_ref, m_ref, l_ref, acc_ref, *, tq, tk, pad):
    i = pl.program_id(1)
    nk = (i * tq + tq + tk - 1) // tk
    r = lax.broadcasted_iota(jnp.int32, (2 * tq, 1), 0)
    rowpos = i * tq + jnp.where(r >= tq, r - tq, r)
    lane = lax.broadcasted_iota(jnp.int32, (1, tk), 1)
    hd = HEAD_DIM
    for kvh in range(FOX_KV_HEADS):
        h0, h1 = 2 * kvh, 2 * kvh + 1
        q2 = jnp.concatenate([q_ref[0, :, h0 * hd:(h0 + 1) * hd], q_ref[0, :, h1 * hd:(h1 + 1) * hd]], axis=0)
        cq2 = jnp.concatenate([cc_ref[0, :, h0:h0 + 1], cc_ref[0, :, h1:h1 + 1]], axis=0)
        _reset(m_ref, l_ref, acc_ref)

        def body(j, carry, kvh=kvh, h0=h0, h1=h1, q2=q2, cq2=cq2):
            ks = pl.multiple_of(j * tk, tk)
            kt = k_ref[0, pl.ds(ks, tk), kvh * hd:(kvh + 1) * hd]
            vt = v_ref[0, pl.ds(ks, tk), kvh * hd:(kvh + 1) * hd]
            s = _dot_nt(q2, kt)
            ck = jnp.concatenate([jnp.broadcast_to(cr_ref[0, h0:h0 + 1, pl.ds(ks, tk)], (tq, tk)),
                                  jnp.broadcast_to(cr_ref[0, h1:h1 + 1, pl.ds(ks, tk)], (tq, tk))], axis=0)
            s = s + cq2 - ck
            col = ks + lane
            s = jnp.where((col <= rowpos) & (col >= pad), s, NEG)
            _softmax_step(s, vt, m_ref, l_ref, acc_ref)
            return carry

        lax.fori_loop(0, nk, body, 0)
        o = acc_ref[...] / l_ref[...]
        o_ref[0, :, h0 * hd:(h0 + 1) * hd] = o[:tq]
        o_ref[0, :, h1 * hd:(h1 + 1) * hd] = o[tq:]


def _fox_prompt(qf, kb, vb, ccol, crow, tq, tk, pad):
    nb, lp, _ = qf.shape
    kern = functools.partial(_fox_prompt_kernel, tq=tq, tk=tk, pad=pad)
    return pl.pallas_call(
        kern,
        out_shape=jax.ShapeDtypeStruct((nb, lp, FOX_WIDTH), F32),
        grid=(nb, lp // tq),
        in_specs=[pl.BlockSpec((1, tq, FOX_WIDTH), lambda b, i: (b, i, 0)),
                  pl.BlockSpec((1, lp, FOX_KV_WIDTH), lambda b, i: (b, 0, 0)),
                  pl.BlockSpec((1, lp, FOX_KV_WIDTH), lambda b, i: (b, 0, 0)),
                  pl.BlockSpec((1, tq, FOX_HEADS), lambda b, i: (b, i, 0)),
                  pl.BlockSpec((1, FOX_HEADS, lp), lambda b, i: (b, 0, 0))],
        out_specs=pl.BlockSpec((1, tq, FOX_WIDTH), lambda b, i: (b, i, 0)),
        scratch_shapes=[pltpu.VMEM((2 * tq, 1), F32), pltpu.VMEM((2 * tq, 1), F32),
                        pltpu.VMEM((2 * tq, HEAD_DIM), F32)],
        compiler_params=pltpu.CompilerParams(dimension_semantics=("parallel", "arbitrary"),
                                             vmem_limit_bytes=VMEM_LIMIT),
        name="fox_prompt",
    )(qf, kb, vb, ccol, crow)


def _diff_lambda(dl_ref, lam_init):
    dl = dl_ref[...]
    a = jnp.sum(dl[0:1] * dl[1:2], axis=-1, keepdims=True)
    b = jnp.sum(dl[2:3] * dl[3:4], axis=-1, keepdims=True)
    return jnp.exp(a) - jnp.exp(b) + lam_init


def _diff_prompt_kernel(q_ref, k_ref, v_ref, dl_ref, o_ref, m1_ref, l1_ref, a1_ref, m2_ref, l2_ref, a2_ref,
                        *, tq, tk, pad, lam_init):
    i = pl.program_id(1)
    nk = (i * tq + tq + tk - 1) // tk
    r = lax.broadcasted_iota(jnp.int32, (2 * tq, 1), 0)
    rowpos = i * tq + jnp.where(r >= tq, r - tq, r)
    lane = lax.broadcasted_iota(jnp.int32, (1, tk), 1)
    lam = _diff_lambda(dl_ref, lam_init)
    hd = HEAD_DIM
    for kvh in range(DIFF_KV_HEADS):
        qo = kvh * 4 * hd
        q1 = jnp.concatenate([q_ref[0, :, qo:qo + hd], q_ref[0, :, qo + 2 * hd:qo + 3 * hd]], axis=0)
        q2 = jnp.concatenate([q_ref[0, :, qo + hd:qo + 2 * hd], q_ref[0, :, qo + 3 * hd:qo + 4 * hd]], axis=0)
        _reset(m1_ref, l1_ref, a1_ref)
        _reset(m2_ref, l2_ref, a2_ref)

        def body(j, carry, kvh=kvh, q1=q1, q2=q2):
            ks = pl.multiple_of(j * tk, tk)
            ko = kvh * 2 * hd
            k1 = k_ref[0, pl.ds(ks, tk), ko:ko + hd]
            k2 = k_ref[0, pl.ds(ks, tk), ko + hd:ko + 2 * hd]
            vt = v_ref[0, pl.ds(ks, tk), ko:ko + 2 * hd]
            col = ks + lane
            mask = (col <= rowpos) & (col >= pad)
            _softmax_step(jnp.where(mask, _dot_nt(q1, k1), NEG), vt, m1_ref, l1_ref, a1_ref)
            _softmax_step(jnp.where(mask, _dot_nt(q2, k2), NEG), vt, m2_ref, l2_ref, a2_ref)
            return carry

        lax.fori_loop(0, nk, body, 0)
        o = a1_ref[...] / l1_ref[...] - lam * a2_ref[...] / l2_ref[...]
        w = 2 * hd
        o_ref[0, :, (2 * kvh) * w:(2 * kvh + 1) * w] = o[:tq]
        o_ref[0, :, (2 * kvh + 1) * w:(2 * kvh + 2) * w] = o[tq:]


def _diff_prompt(dq, kb, vb, dlam, tq, tk, pad, lam_init):
    nb, lp, _ = dq.shape
    kern = functools.partial(_diff_prompt_kernel, tq=tq, tk=tk, pad=pad, lam_init=lam_init)
    stat = pltpu.VMEM((2 * tq, 1), F32)
    acc = pltpu.VMEM((2 * tq, 2 * HEAD_DIM), F32)
    return pl.pallas_call(
        kern,
        out_shape=jax.ShapeDtypeStruct((nb, lp, DIFF_WIDTH), F32),
        grid=(nb, lp // tq),
        in_specs=[pl.BlockSpec((1, tq, DIFF_WIDTH), lambda b, i: (b, i, 0)),
                  pl.BlockSpec((1, lp, DIFF_KV_WIDTH), lambda b, i: (b, 0, 0)),
                  pl.BlockSpec((1, lp, DIFF_KV_WIDTH), lambda b, i: (b, 0, 0)),
                  pl.BlockSpec((4, HEAD_DIM), lambda b, i: (0, 0))],
        out_specs=pl.BlockSpec((1, tq, DIFF_WIDTH), lambda b, i: (b, i, 0)),
        scratch_shapes=[stat, stat, acc, stat, stat, acc],
        compiler_params=pltpu.CompilerParams(dimension_semantics=("parallel", "arbitrary"),
                                             vmem_limit_bytes=VMEM_LIMIT),
        name="diff_prompt",
    )(dq, kb, vb, dlam)


def _pad_rows(x, rows):
    return jnp.concatenate([x, jnp.zeros((rows - x.shape[0], x.shape[1]), x.dtype)], axis=0)


def _fox_sample_kernel(pt_ref, qbd_ref, lcol_ref, lrow_ref, knew_ref, vnew_ref, sup_ref, *rest, pp, t_new):
    k_refs, v_refs, lp_refs = rest[:pp], rest[pp:2 * pp], rest[2 * pp:3 * pp]
    o_ref, m_ref, l_ref, acc_ref, carry_ref = rest[3 * pp:]
    j = pl.program_id(1)
    nh = FOX_HEADS
    rows = t_new * nh

    @pl.when(j == 0)
    def _():
        _reset(m_ref, l_ref, acc_ref)
        carry_ref[...] = jnp.zeros_like(carry_ref)

    qbd = qbd_ref[0]
    lc = lcol_ref[0]
    parts, run = [], None
    for t in range(t_new):
        run = lc[t * nh:(t + 1) * nh] if run is None else run + lc[t * nh:(t + 1) * nh]
        parts.append(run)
    qc = jnp.concatenate(parts, axis=0)

    lp_all = jnp.concatenate([lp_refs[s][0] for s in range(pp)], axis=0)
    hi, mid, lo = _split3(lp_all)
    sup = sup_ref[...]
    r_in = _dot(hi, sup) + _dot(mid, sup) + _dot(lo, sup)
    tot = r_in[:, 0:1] + lp_all[:, 0:1]
    c = carry_ref[:, 0:1]
    s_parts = []
    for s in range(pp):
        r_s = r_in[s * nh:(s + 1) * nh] + c
        c = c + tot[s * nh:(s + 1) * nh]
        bias = jnp.concatenate([r_s] * t_new, axis=0) + qc
        s_parts.append(_dot_nt(qbd, k_refs[s][0].astype(BF16)) + bias)
    carry_ref[:, 0:1] = c
    sc = jnp.concatenate(s_parts, axis=1)

    m_prev = m_ref[...]
    m_new = jnp.maximum(m_prev, jnp.max(sc, axis=-1, keepdims=True))
    alpha = jnp.exp(m_prev - m_new)
    p = jnp.exp(sc - m_new)
    l_ref[...] = alpha * l_ref[...] + jnp.sum(p, axis=-1, keepdims=True)
    page = k_refs[0].shape[1]
    pv = None
    for s in range(pp):
        d = _dot(p[:, s * page:(s + 1) * page].astype(BF16), v_refs[s][0].astype(BF16))
        pv = d if pv is None else pv + d
    acc_ref[...] = alpha * acc_ref[...] + pv
    m_ref[...] = m_new

    @pl.when(j == pl.num_programs(1) - 1)
    def _():
        kn = _pad_rows(knew_ref[0], page).astype(BF16)
        vn = _pad_rows(vnew_ref[0], page).astype(BF16)
        lr = lrow_ref[0]
        lane8 = lax.broadcasted_iota(jnp.int32, lr.shape, 1)
        qct = jnp.zeros(lr.shape, F32)
        for t in range(t_new):
            qct = qct + jnp.where(lane8 >= t, lr[:, t:t + 1], 0.0)
        rr = lax.broadcasted_iota(jnp.int32, (rows, page), 0)
        cc = lax.broadcasted_iota(jnp.int32, (rows, page), 1)
        s_new = _dot_nt(qbd, kn) + qc - jnp.concatenate([qct] * t_new, axis=0)
        s_new = jnp.where((cc * nh <= rr) & (cc < t_new), s_new, NEG)
        _softmax_step(s_new, vn, m_ref, l_ref, acc_ref)
        o = acc_ref[...] / l_ref[...]
        o_a, o_b = o[:, :LANES], o[:, LANES:]
        kv = (lax.broadcasted_iota(jnp.int32, (rows, LANES), 0) % nh) // (FOX_HEADS // FOX_KV_HEADS)
        sel = jnp.where(kv == 0, o_a, jnp.where(kv == 1, pltpu.roll(o_a, HEAD_DIM, 1),
                        jnp.where(kv == 2, o_b, pltpu.roll(o_b, HEAD_DIM, 1))))
        o_ref[0] = sel[:, :HEAD_DIM]


def _page_specs(n, block, pp, n_pages):
    def spec(s):
        return pl.BlockSpec(block, lambda b, j, pt: (pt[b * n_pages + n_pages - 1 - (j * pp + s)], 0, 0))
    return [spec(s) for s in range(n)]


def _fox_sample(pt_flat, qbd, lcol, lrow, knew, vnew, ck, cv, clf, n_pages, pp):
    db, rows, _ = qbd.shape
    t_new = rows // FOX_HEADS
    page = ck.shape[1]
    sup = jnp.asarray(np.tril(np.ones((page, page), np.float32), -1), dtype=BF16)
    per_b = lambda shape: pl.BlockSpec((1,) + shape, lambda b, j, pt: (b, 0, 0))
    in_specs = ([per_b((rows, MXU_DIM)), per_b((rows, 1)), per_b((FOX_HEADS, LANES)),
                 per_b((8, FOX_KV_WIDTH)), per_b((8, FOX_KV_WIDTH)),
                 pl.BlockSpec((page, page), lambda b, j, pt: (0, 0))]
                + _page_specs(pp, (1, page, FOX_KV_WIDTH), pp, n_pages)
                + _page_specs(pp, (1, page, FOX_KV_WIDTH), pp, n_pages)
                + _page_specs(pp, (1, FOX_HEADS, page), pp, n_pages))
    kern = functools.partial(_fox_sample_kernel, pp=pp, t_new=t_new)
    return pl.pallas_call(
        kern,
        out_shape=jax.ShapeDtypeStruct((db, rows, HEAD_DIM), F32),
        grid_spec=pltpu.PrefetchScalarGridSpec(
            num_scalar_prefetch=1,
            grid=(db, n_pages // pp),
            in_specs=in_specs,
            out_specs=pl.BlockSpec((1, rows, HEAD_DIM), lambda b, j, pt: (b, 0, 0)),
            scratch_shapes=[pltpu.VMEM((rows, 1), F32), pltpu.VMEM((rows, 1), F32),
                            pltpu.VMEM((rows, FOX_KV_WIDTH), F32), pltpu.VMEM((FOX_HEADS, LANES), F32)]),
        compiler_params=pltpu.CompilerParams(dimension_semantics=("parallel", "arbitrary"),
                                             vmem_limit_bytes=VMEM_LIMIT),
        name="fox_sample",
    )(pt_flat, qbd, lcol, lrow, knew, vnew, sup, *([ck] * pp), *([cv] * pp), *([clf] * pp))


def _diff_sample_kernel(pt_ref, qbd_ref, knew_ref, vnew_ref, dl_ref, *rest, pp, t_new, lam_init):
    k_refs, v_refs = rest[:pp], rest[pp:2 * pp]
    o_ref, m_ref, l_ref, acc_ref = rest[2 * pp:]
    j = pl.program_id(1)
    half = t_new * DIFF_HEADS
    rows = 2 * half

    @pl.when(j == 0)
    def _():
        _reset(m_ref, l_ref, acc_ref)

    qbd = qbd_ref[0]
    page = k_refs[0].shape[1]
    sc = jnp.concatenate([_dot_nt(qbd, k_refs[s][0].astype(BF16)) for s in range(pp)], axis=1)
    m_prev = m_ref[...]
    m_new = jnp.maximum(m_prev, jnp.max(sc, axis=-1, keepdims=True))
    alpha = jnp.exp(m_prev - m_new)
    p = jnp.exp(sc - m_new)
    l_ref[...] = alpha * l_ref[...] + jnp.sum(p, axis=-1, keepdims=True)
    pv = None
    for s in range(pp):
        d = _dot(p[:, s * page:(s + 1) * page].astype(BF16), v_refs[s][0].astype(BF16))
        pv = d if pv is None else pv + d
    acc_ref[...] = alpha * acc_ref[...] + pv
    m_ref[...] = m_new

    @pl.when(j == pl.num_programs(1) - 1)
    def _():
        kn = _pad_rows(knew_ref[0], page).astype(BF16)
        vn = _pad_rows(vnew_ref[0], page).astype(BF16)
        rr = lax.broadcasted_iota(jnp.int32, (rows, page), 0)
        cc = lax.broadcasted_iota(jnp.int32, (rows, page), 1)
        t_row = (rr % half) // DIFF_HEADS
        s_new = jnp.where((cc <= t_row) & (cc < t_new), _dot_nt(qbd, kn), NEG)
        _softmax_step(s_new, vn, m_ref, l_ref, acc_ref)
        o = acc_ref[...] / l_ref[...]
        kv = (lax.broadcasted_iota(jnp.int32, (rows, LANES), 0) % DIFF_HEADS) // (DIFF_HEADS // DIFF_KV_HEADS)
        sel = jnp.where(kv == 0, o[:, :LANES], o[:, LANES:])
        o_ref[0] = sel[:half] - _diff_lambda(dl_ref, lam_init) * sel[half:]


def _diff_sample(pt_flat, qbd, knew, vnew, dlam, ck, cv, n_pages, pp, lam_init):
    db, rows, _ = qbd.shape
    t_new = rows // (2 * DIFF_HEADS)
    page = ck.shape[1]
    per_b = lambda shape: pl.BlockSpec((1,) + shape, lambda b, j, pt: (b, 0, 0))
    in_specs = ([per_b((rows, MXU_DIM)), per_b((8, DIFF_KV_WIDTH)), per_b((8, DIFF_KV_WIDTH)),
                 pl.BlockSpec((4, HEAD_DIM), lambda b, j, pt: (0, 0))]
                + _page_specs(pp, (1, page, DIFF_KV_WIDTH), pp, n_pages)
                + _page_specs(pp, (1, page, DIFF_KV_WIDTH), pp, n_pages))
    kern = functools.partial(_diff_sample_kernel, pp=pp, t_new=t_new, lam_init=lam_init)
    return pl.pallas_call(
        kern,
        out_shape=jax.ShapeDtypeStruct((db, rows // 2, 2 * HEAD_DIM), F32),
        grid_spec=pltpu.PrefetchScalarGridSpec(
            num_scalar_prefetch=1,
            grid=(db, n_pages // pp),
            in_specs=in_specs,
            out_specs=pl.BlockSpec((1, rows // 2, 2 * HEAD_DIM), lambda b, j, pt: (b, 0, 0)),
            scratch_shapes=[pltpu.VMEM((rows, 1), F32), pltpu.VMEM((rows, 1), F32),
                            pltpu.VMEM((rows, DIFF_KV_WIDTH), F32)]),
        compiler_params=pltpu.CompilerParams(dimension_semantics=("parallel", "arbitrary"),
                                             vmem_limit_bytes=VMEM_LIMIT),
        name="diff_sample",
    )(pt_flat, qbd, knew, vnew, dlam, *([ck] * pp), *([cv] * pp))


def _ffn_kernel(*refs, tm, d_ff, post_scale, stateful):
    if stateful:
        (x_ref, fo_ref, do_ref, gm64_ref, gm128_ref, gfo_ref, gdo_ref, wo_ref, gffn_ref, wup_ref,
         cw_ref, cb_ref, wdn_ref, s0_ref, s1_ref, y_ref, h_ref) = refs
    else:
        (x_ref, fo_ref, do_ref, gm64_ref, gm128_ref, gfo_ref, gdo_ref, wo_ref, gffn_ref, wup_ref,
         cw_ref, cb_ref, wdn_ref, y_ref, h_ref, carry_ref) = refs

        @pl.when(pl.program_id(1) == 0)
        def _():
            carry_ref[...] = jnp.zeros_like(carry_ref)

    fo = _group_rmsnorm(fo_ref[0], gm64_ref[...], gfo_ref[...])
    do = _group_rmsnorm(do_ref[0], gm128_ref[...], gdo_ref[...]) * post_scale
    att = _dot(jnp.concatenate([fo, do], axis=1).astype(BF16), wo_ref[...])
    x1 = x_ref[0] + att
    hn = x1 * lax.rsqrt(jnp.mean(x1 * x1, axis=-1, keepdims=True) + EPS) * gffn_ref[...]
    hnb = hn.astype(BF16)
    ch = FF_CHUNK
    rid = lax.broadcasted_iota(jnp.int32, (8, ch), 0)
    acc = x1
    for c in range(d_ff // ch):
        conv = []
        for half in range(2):
            col = half * d_ff + c * ch
            h = _dot(hnb, wup_ref[:, col:col + ch])
            if stateful:
                nseq = s0_ref.shape[0]
                hm1 = jnp.concatenate([s1_ref[:, col:col + ch], h[:tm - nseq]], axis=0)
                hm2 = jnp.concatenate([s0_ref[:, col:col + ch], s1_ref[:, col:col + ch], h[:tm - 2 * nseq]], axis=0)
                h_ref[:, col:col + ch] = h[tm - 2 * nseq:]
            else:
                prev = carry_ref[:, col:col + ch]
                r1, r2 = pltpu.roll(h, 1, 0), pltpu.roll(h, 2, 0)
                top1 = jnp.where(rid < 1, pltpu.roll(prev, 1, 0), r1[0:8])
                top2 = jnp.where(rid < 2, pltpu.roll(prev, 2, 0), r2[0:8])
                hm1 = jnp.concatenate([top1, r1[8:]], axis=0)
                hm2 = jnp.concatenate([top2, r2[8:]], axis=0)
                carry_ref[:, col:col + ch] = h[tm - 8:]
                h_ref[0, :, col:col + ch] = h[tm - 8:]
            conv.append(cb_ref[:, col:col + ch] + cw_ref[0:1, col:col + ch] * hm2
                        + cw_ref[1:2, col:col + ch] * hm1 + cw_ref[2:3, col:col + ch] * h)
        g, u = conv
        act = g * (1.0 / (1.0 + jnp.exp(-g))) * u
        acc = acc + _dot(act.astype(BF16), wdn_ref[c * ch:(c + 1) * ch, :])
    y_ref[0] = acc


def _merge_ffn(x, fo, do, p, tm, state=None):
    nb, lt, d = x.shape
    d_ff = p["w_down"].shape[0]
    stateful = state is not None
    single = pl.Buffered(1)
    row = lambda w: pl.BlockSpec((1, tm, w), lambda b, i: (b, i, 0), pipeline_mode=single if stateful else None)
    const = lambda shape: pl.BlockSpec(shape, lambda b, i: (0,) * len(shape), pipeline_mode=single)
    in_specs = [row(d), row(FOX_WIDTH), row(DIFF_WIDTH), const((MXU_DIM, MXU_DIM)), const((MXU_DIM, MXU_DIM)),
                const((1, FOX_WIDTH)), const((1, DIFF_WIDTH)), const((FOX_WIDTH + DIFF_WIDTH, d)), const((1, d)),
                const((d, 2 * d_ff)), const((CONV_W, 2 * d_ff)), const((1, 2 * d_ff)), const((d_ff, d))]
    args = [x, fo, do, p["gm64"], p["gm128"], p["g_fo"], p["g_do"], p["w_o"], p["g_ffn"], p["w_up"],
            p["conv_w"], p["conv_b"], p["w_down"]]
    if stateful:
        nseq = state[0].shape[0]
        in_specs += [const((nseq, 2 * d_ff)), const((nseq, 2 * d_ff))]
        args += list(state)
        h_shape = jax.ShapeDtypeStruct((2 * nseq, 2 * d_ff), F32)
        h_spec = pl.BlockSpec((2 * nseq, 2 * d_ff), lambda b, i: (0, 0))
        scratch = []
    else:
        h_shape = jax.ShapeDtypeStruct((nb, 8, 2 * d_ff), F32)
        h_spec = pl.BlockSpec((1, 8, 2 * d_ff), lambda b, i: (b, 0, 0))
        scratch = [pltpu.VMEM((8, 2 * d_ff), F32)]
    kern = functools.partial(_ffn_kernel, tm=tm, d_ff=d_ff, post_scale=p["post_scale"], stateful=stateful)
    return pl.pallas_call(
        kern,
        out_shape=[jax.ShapeDtypeStruct((nb, lt, d), F32), h_shape],
        grid=(nb, lt // tm),
        in_specs=in_specs,
        out_specs=[pl.BlockSpec((1, tm, d), lambda b, i: (b, i, 0)), h_spec],
        scratch_shapes=scratch,
        compiler_params=pltpu.CompilerParams(dimension_semantics=("parallel", "arbitrary"),
                                             vmem_limit_bytes=VMEM_LIMIT),
        name="merge_ffn_sample" if stateful else "merge_ffn_prompt",
    )(*args)


def kernel(x_prompt, x_sample, cache_fox_k, cache_fox_v, cache_fox_logf, cache_diff_k, cache_diff_v, state_ffn_conv, page_table, meta_tokens, attn_norm_g, w_in, b_f, fox_qn_g, fox_kn_g, fox_on_g, diff_qn_g, diff_kn_g, diff_lambda, diff_subln_g, w_o, ffn_norm_g, w_up, conv_w, conv_b, w_down):
    nb, seq, d = x_prompt.shape
    db, t_new, _ = x_sample.shape
    depth, n_phys, page = cache_fox_k.shape[:3]
    n_pages = page_table.shape[1]
    assert depth == 1, "single layer only"
    layer = 0
    lam_init = 0.8 - 0.6 * math.exp(-0.3 * layer)
    past_len = n_pages * page

    wi = w_in[layer]
    o = np.cumsum([0, FOX_WIDTH, FOX_KV_WIDTH, FOX_KV_WIDTH, FOX_HEADS, DIFF_WIDTH, DIFF_KV_WIDTH, DIFF_KV_WIDTH])
    seg = lambda k: wi[:, o[k]:o[k + 1]]
    w_all = jnp.concatenate([seg(0), seg(1), seg(2), seg(4), seg(5), seg(6),
                             jnp.pad(seg(3), ((0, 0), (0, LANES - FOX_HEADS)))], axis=1).astype(BF16)
    p = {
        "g_attn": attn_norm_g[layer][None, :],
        "w_in": w_all,
        "gm64": _block_diag_mean(HEAD_DIM),
        "gm128": _block_diag_mean(2 * HEAD_DIM),
        "g_heads": jnp.concatenate([jnp.tile(fox_qn_g[layer], FOX_HEADS), jnp.tile(fox_kn_g[layer], FOX_KV_HEADS),
                                    jnp.tile(diff_qn_g[layer], 2 * DIFF_HEADS),
                                    jnp.tile(diff_kn_g[layer], 2 * DIFF_KV_HEADS)])[None, :],
        "b_f": jnp.pad(b_f[layer], (0, LANES - FOX_HEADS))[None, :],
        "g_fo": fox_on_g[layer].reshape(1, FOX_WIDTH),
        "g_do": jnp.tile(diff_subln_g[layer], DIFF_HEADS)[None, :],
        "post_scale": 1.0 - lam_init,
        "w_o": w_o[layer].astype(BF16),
        "g_ffn": ffn_norm_g[layer][None, :],
        "w_up": w_up[layer].astype(BF16),
        "conv_w": conv_w[layer],
        "conv_b": conv_b[layer][None, :],
        "w_down": w_down[layer].astype(BF16),
    }
    dlam = diff_lambda[layer]

    tm = ROW_TILE
    pad = (-N_META) % tm
    lp = pad + N_META + seq
    xp = jnp.concatenate([jnp.zeros((nb, pad, d), x_prompt.dtype),
                          jnp.broadcast_to(meta_tokens[None].astype(x_prompt.dtype), (nb, N_META, d)),
                          x_prompt], axis=1)
    pos_p = jnp.arange(lp) - pad
    (qf, fk, fv, lf, dk, dv, fkb, fvb, dqb, dkb, dvb, ccol, crow) = _project(xp, pos_p, tm, p)
    fo = _fox_prompt(qf, fkb, fvb, ccol, crow, tm, tm, pad)
    do = _diff_prompt(dqb, dkb, dvb, dlam, tm, tm, pad, lam_init)
    yp, hlast = _merge_ffn(xp, fo, do, p, tm)
    ltot = N_META + seq
    y_prompt = yp[:, pad + N_META:]
    pk = fk[:, pad:].reshape(1, nb, ltot, FOX_KV_HEADS, HEAD_DIM)
    pv = fv[:, pad:].reshape(1, nb, ltot, FOX_KV_HEADS, HEAD_DIM)
    pf = lf[:, pad:][None]
    pdk = dk[:, pad:].reshape(1, nb, ltot, DIFF_KV_HEADS, 2, HEAD_DIM)
    pdv = dv[:, pad:].reshape(1, nb, ltot, DIFF_KV_HEADS, 2 * HEAD_DIM)
    pc = hlast[:, 8 - (CONV_W - 1):][None]

    rows = db * t_new
    pos_s = past_len + (jnp.arange(rows) % t_new)
    (qf, fk, fv, lf, dk, dv, _, _, dqb, _, _, _, _) = _project(x_sample.reshape(1, rows, d), pos_s, ROW_TILE, p)
    g_f = FOX_HEADS // FOX_KV_HEADS
    g_d = DIFF_HEADS // DIFF_KV_HEADS
    eye_f = jnp.eye(FOX_KV_HEADS, dtype=BF16)
    qbd_f = jnp.einsum('btkgd,kj->btkgjd', qf.reshape(db, t_new, FOX_KV_HEADS, g_f, HEAD_DIM), eye_f)
    qbd_f = qbd_f.reshape(db, t_new * FOX_HEADS, FOX_KV_WIDTH)
    eye_d, eye_2 = jnp.eye(DIFF_KV_HEADS, dtype=BF16), jnp.eye(2, dtype=BF16)
    qbd_d = jnp.einsum('btkgwd,kj,wv->bwtkgjvd', dqb.reshape(db, t_new, DIFF_KV_HEADS, g_d, 2, HEAD_DIM), eye_d, eye_2)
    qbd_d = qbd_d.reshape(db, 2 * t_new * DIFF_HEADS, DIFF_KV_WIDTH)
    lf_s = lf.reshape(db, t_new, FOX_HEADS)
    lcol = lf_s.reshape(db, t_new * FOX_HEADS, 1)
    lrow = jnp.pad(jnp.swapaxes(lf_s, 1, 2), ((0, 0), (0, 0), (0, LANES - t_new)))
    new_rows = lambda a: jnp.pad(a.reshape(db, t_new, a.shape[-1]), ((0, 0), (0, 8 - t_new), (0, 0)))
    pt_flat = page_table.reshape(-1).astype(jnp.int32)
    ck = cache_fox_k[layer].reshape(n_phys, page, FOX_KV_WIDTH)
    cv = cache_fox_v[layer].reshape(n_phys, page, FOX_KV_WIDTH)
    clf = jnp.swapaxes(cache_fox_logf[layer], 1, 2)
    fo_s = _fox_sample(pt_flat, qbd_f, lcol, lrow, new_rows(fk), new_rows(fv), ck, cv, clf, n_pages, PAGES_PER_STEP)
    cdk = cache_diff_k[layer].reshape(n_phys, page, DIFF_KV_WIDTH)
    cdv = cache_diff_v[layer].reshape(n_phys, page, DIFF_KV_WIDTH)
    do_s = _diff_sample(pt_flat, qbd_d, new_rows(dk), new_rows(dv), dlam, cdk, cdv, n_pages, PAGES_PER_STEP, lam_init)

    tmajor = lambda a, w: jnp.swapaxes(a.reshape(db, t_new, w), 0, 1).reshape(1, rows, w)
    st = state_ffn_conv[layer]
    ys, hs = _merge_ffn(tmajor(x_sample, d), tmajor(fo_s, FOX_WIDTH), tmajor(do_s, DIFF_WIDTH), p, rows,
                        state=(st[:, 0], st[:, 1]))
    y_sample = jnp.swapaxes(ys.reshape(t_new, db, d), 0, 1)
    sc = jnp.swapaxes(hs.reshape(CONV_W - 1, db, hs.shape[-1]), 0, 1)[None]
    sk = fk.reshape(1, db, t_new, FOX_KV_HEADS, HEAD_DIM)
    sv = fv.reshape(1, db, t_new, FOX_KV_HEADS, HEAD_DIM)
    sf = lf_s[None]
    sdk = dk.reshape(1, db, t_new, DIFF_KV_HEADS, 2, HEAD_DIM)
    sdv = dv.reshape(1, db, t_new, DIFF_KV_HEADS, 2 * HEAD_DIM)
    return (y_prompt, y_sample, pk, pv, pf, pdk, pdv, pc, sk, sv, sf, sdk, sdv, sc)
```

```python
import functools
import math

import numpy as np
import jax
import jax.numpy as jnp
from jax import lax
from jax.experimental import pallas as pl
from jax.experimental.pallas import tpu as pltpu

F32 = jnp.float32
BF16 = jnp.bfloat16

HEAD_DIM = 64
FOX_HEADS = 8
FOX_KV_HEADS = 4
DIFF_HEADS = 4
DIFF_KV_HEADS = 2
FOX_WIDTH = FOX_HEADS * HEAD_DIM
FOX_KV_WIDTH = FOX_KV_HEADS * HEAD_DIM
DIFF_WIDTH = DIFF_HEADS * 2 * HEAD_DIM
DIFF_KV_WIDTH = DIFF_KV_HEADS * 2 * HEAD_DIM
ROT_DIM = HEAD_DIM // 4
ROPE_THETA = 500000.0
CONV_W = 3
N_META = 16
EPS = 1e-6
NEG = -1e30
SCALE = HEAD_DIM ** -0.5
LOG2E = 1.4426950408889634

LANES = 128
MXU_DIM = 256
ROW_TILE = 256
FF_CHUNK = 256
PAGE_UNROLL = 8
VMEM_LIMIT = 56 * 1024 * 1024

_FQ, _FK, _FV, _DQ, _DK, _DV, _FF, _W_TOTAL = 0, 512, 768, 1024, 1536, 1792, 2048, 2176
_AUG_C, _AUG_G0 = HEAD_DIM, HEAD_DIM + 3


def _split3(x):
    hi = x.astype(BF16)
    r = x - hi.astype(F32)
    mid = r.astype(BF16)
    lo = (r - mid.astype(F32)).astype(BF16)
    return hi, mid, lo


def _dot(a, b):
    return jnp.dot(a, b, preferred_element_type=F32)


def _dot_nt(a, b):
    return lax.dot_general(a, b, (((1,), (1,)), ((), ())), preferred_element_type=F32)


def _group_mean_sq(x, gmat):
    outs = []
    for c in range(x.shape[1] // MXU_DIM):
        s = x[:, c * MXU_DIM:(c + 1) * MXU_DIM]
        outs.append(_dot((s * s).astype(BF16), gmat))
    return outs[0] if len(outs) == 1 else jnp.concatenate(outs, axis=1)


def _group_rmsnorm(x, gmat, gain):
    return x * lax.rsqrt(_group_mean_sq(x, gmat) + EPS) * gain


def _proj_kernel(*refs, prompt):
    (x_ref, gattn_ref, w_ref, gm_ref, gh_ref, bf_ref, cos_ref, sa_ref, sb_ref) = refs[:9]
    if prompt:
        (tri_ref, pq_ref, pk_ref, oq_ref, ok_ref,
         fk_ref, fv_ref, lf_ref, dk_ref, dv_ref,
         qat_ref, ka_ref, vt_ref, dqt_ref, dka_ref, dvt_ref, carry_ref) = refs[9:]
    else:
        (qf_ref, fk_ref, fv_ref, lf_ref, dqb_ref, dk_ref, dv_ref) = refs[9:]
    x = x_ref[0]
    xn = x * lax.rsqrt(jnp.mean(x * x, axis=-1, keepdims=True) + EPS) * gattn_ref[...]
    y = _dot(xn.astype(BF16), w_ref[...])
    gm = gm_ref[...]
    cos_t, sin_a, sin_b = cos_ref[...], sa_ref[...], sb_ref[...]

    def rope(seg):
        outs = []
        for c in range(seg.shape[1] // LANES):
            s = seg[:, c * LANES:(c + 1) * LANES]
            outs.append(s * cos_t + pltpu.roll(s, LANES - ROT_DIM // 2, 1) * sin_a
                        + pltpu.roll(s, ROT_DIM // 2, 1) * sin_b)
        return jnp.concatenate(outs, axis=1)

    fq = _group_rmsnorm(y[:, _FQ:_FK], gm, gh_ref[:, 0:512])
    fk = _group_rmsnorm(y[:, _FK:_FV], gm, gh_ref[:, 512:768])
    fv = y[:, _FV:_DQ]
    dq = rope(_group_rmsnorm(y[:, _DQ:_DK], gm, gh_ref[:, 768:1280]))
    dk = rope(_group_rmsnorm(y[:, _DK:_DV], gm, gh_ref[:, 1280:1536]))
    dv = y[:, _DV:_FF]
    z = y[:, _FF:_W_TOTAL] + bf_ref[...]
    lf = jnp.minimum(z, 0.0) - jnp.log1p(jnp.exp(-jnp.abs(z)))
    fk_ref[0] = fk
    fv_ref[0] = fv
    dk_ref[0] = dk
    dv_ref[0] = dv
    lf_ref[0] = lf[:, :FOX_HEADS]
    if not prompt:
        qf_ref[0] = (fq * SCALE).astype(BF16)
        dqb_ref[0] = (dq * SCALE).astype(BF16)
        return

    lane = lax.broadcasted_iota(jnp.int32, lf.shape, 1)
    lf = jnp.where(lane < FOX_HEADS, lf, 0.0)
    hi, mid, lo = _split3(lf)
    tri = tri_ref[...]
    cs = _dot(tri, hi) + _dot(tri, mid) + _dot(tri, lo)

    @pl.when(pl.program_id(1) == 0)
    def _():
        carry_ref[...] = jnp.zeros_like(carry_ref)

    c = cs + carry_ref[0:1, :]
    tm = c.shape[0]
    carry_ref[0:1, :] = c[tm - 1:tm, :]

    x3 = jnp.concatenate(_split3(c * LOG2E), axis=1)
    eq = _dot(x3, pq_ref[...]) + oq_ref[...]
    ek = _dot(x3, pk_ref[...]) + ok_ref[...]
    low = lane < HEAD_DIM
    half = lambda chunk, odd: pltpu.roll(chunk, HEAD_DIM, 1) if odd else chunk
    fqs = fq * (SCALE * LOG2E)
    for h in range(FOX_HEADS):
        chunk = fqs[:, (h // 2) * LANES:(h // 2 + 1) * LANES]
        qa = jnp.where(low, half(chunk, h % 2), eq[:, h * LANES:(h + 1) * LANES])
        qat_ref[0, h * LANES:(h + 1) * LANES, :] = qa.T.astype(BF16)
    for h in range(FOX_KV_HEADS):
        chunk = fk[:, (h // 2) * LANES:(h // 2 + 1) * LANES]
        ka = jnp.where(low, half(chunk, h % 2), ek[:, h * LANES:(h + 1) * LANES])
        ka_ref[0, :, h * LANES:(h + 1) * LANES] = ka.astype(BF16)
    for c2 in range(FOX_KV_WIDTH // LANES):
        vt_ref[0, c2 * LANES:(c2 + 1) * LANES, :] = fv[:, c2 * LANES:(c2 + 1) * LANES].T.astype(BF16)
    dqs = dq * (SCALE * LOG2E)
    for c2 in range(DIFF_WIDTH // LANES):
        dqt_ref[0, c2 * LANES:(c2 + 1) * LANES, :] = dqs[:, c2 * LANES:(c2 + 1) * LANES].T.astype(BF16)
    for h in range(2 * DIFF_KV_HEADS):
        chunk = dk[:, (h // 2) * LANES:(h // 2 + 1) * LANES]
        dka_ref[0, :, h * LANES:(h + 1) * LANES] = jnp.where(low, half(chunk, h % 2), 0.0).astype(BF16)
    for c2 in range(DIFF_KV_WIDTH // LANES):
        dvt_ref[0, c2 * LANES:(c2 + 1) * LANES, :] = dv[:, c2 * LANES:(c2 + 1) * LANES].T.astype(BF16)


def _rope_tables(pos):
    inv_freq = ROPE_THETA ** (-jnp.arange(0, ROT_DIM, 2, dtype=F32) / ROT_DIM)
    ang = pos.astype(F32)[:, None] * inv_freq[None, :]
    cos, sin = jnp.cos(ang), jnp.sin(ang)
    half = ROT_DIM // 2
    ones = jnp.ones((pos.shape[0], HEAD_DIM - ROT_DIM), F32)
    zeros_h = jnp.zeros((pos.shape[0], half), F32)
    zeros_r = jnp.zeros((pos.shape[0], HEAD_DIM - ROT_DIM), F32)
    cos_h = jnp.concatenate([cos, cos, ones], axis=1)
    sa_h = jnp.concatenate([-sin, zeros_h, zeros_r], axis=1)
    sb_h = jnp.concatenate([zeros_h, sin, zeros_r], axis=1)
    rep = LANES // HEAD_DIM
    return jnp.tile(cos_h, (1, rep)), jnp.tile(sa_h, (1, rep)), jnp.tile(sb_h, (1, rep))


def _block_diag_mean(group):
    idx = np.arange(MXU_DIM) // group
    return jnp.asarray((idx[:, None] == idx[None, :]).astype(np.float32) / group, dtype=BF16)


def _bias_placement():
    g = FOX_HEADS // FOX_KV_HEADS
    pq = np.zeros((3 * LANES, FOX_HEADS * LANES), np.float32)
    pk = np.zeros((3 * LANES, FOX_KV_HEADS * LANES), np.float32)
    oq = np.zeros((1, FOX_HEADS * LANES), np.float32)
    ok = np.zeros((1, FOX_KV_HEADS * LANES), np.float32)
    for h in range(FOX_HEADS):
        kvh, gi = divmod(h, g)
        for piece in range(3):
            pq[piece * LANES + h, h * LANES + _AUG_C + piece] = 1.0
            pk[piece * LANES + h, kvh * LANES + _AUG_G0 + 3 * gi + piece] = -1.0
            oq[0, h * LANES + _AUG_G0 + 3 * gi + piece] = 1.0
            ok[0, kvh * LANES + _AUG_C + piece] = 1.0
    return jnp.asarray(pq, BF16), jnp.asarray(pk, BF16), jnp.asarray(oq), jnp.asarray(ok)


def _project(x, pos, tm, p, prompt):
    nb, lt, d = x.shape
    nt = lt // tm
    cos_t, sin_a, sin_b = _rope_tables(pos)
    row = lambda w, dt=None: pl.BlockSpec((1, tm, w), lambda b, i: (b, i, 0))
    col = lambda w: pl.BlockSpec((1, w, tm), lambda b, i: (b, 0, i))
    const = lambda shape: pl.BlockSpec(shape, lambda b, i: (0,) * len(shape))
    tab = pl.BlockSpec((tm, LANES), lambda b, i: (i, 0))
    sds = jax.ShapeDtypeStruct
    in_specs = [row(d), const((1, d)), const((d, _W_TOTAL)), const((MXU_DIM, MXU_DIM)),
                const((1, 1536)), const((1, LANES)), tab, tab, tab]
    args = [x, p["g_attn"], p["w_in"], p["gm64"], p["g_heads"], p["b_f"], cos_t, sin_a, sin_b]
    f32_rows = [sds((nb, lt, FOX_KV_WIDTH), F32), sds((nb, lt, FOX_KV_WIDTH), F32), sds((nb, lt, FOX_HEADS), F32),
                sds((nb, lt, DIFF_KV_WIDTH), F32), sds((nb, lt, DIFF_KV_WIDTH), F32)]
    f32_specs = [row(FOX_KV_WIDTH), row(FOX_KV_WIDTH), row(FOX_HEADS), row(DIFF_KV_WIDTH), row(DIFF_KV_WIDTH)]
    if prompt:
        tri = jnp.asarray(np.tril(np.ones((tm, tm), np.float32)), dtype=BF16)
        pq, pk, oq, ok = _bias_placement()
        in_specs += [const((tm, tm)), const(pq.shape), const(pk.shape), const(oq.shape), const(ok.shape)]
        args += [tri, pq, pk, oq, ok]
        out_shape = f32_rows + [sds((nb, FOX_HEADS * LANES, lt), BF16), sds((nb, lt, FOX_KV_HEADS * LANES), BF16),
                                sds((nb, FOX_KV_WIDTH, lt), BF16), sds((nb, DIFF_WIDTH, lt), BF16),
                                sds((nb, lt, 2 * DIFF_KV_HEADS * LANES), BF16), sds((nb, DIFF_KV_WIDTH, lt), BF16)]
        out_specs = f32_specs + [col(FOX_HEADS * LANES), row(FOX_KV_HEADS * LANES), col(FOX_KV_WIDTH),
                                 col(DIFF_WIDTH), row(2 * DIFF_KV_HEADS * LANES), col(DIFF_KV_WIDTH)]
        scratch = [pltpu.VMEM((8, LANES), F32)]
    else:
        out_shape = ([sds((nb, lt, FOX_WIDTH), BF16)] + f32_rows[:3] + [sds((nb, lt, DIFF_WIDTH), BF16)] + f32_rows[3:])
        out_specs = [row(FOX_WIDTH)] + f32_specs[:3] + [row(DIFF_WIDTH)] + f32_specs[3:]
        scratch = []
    return pl.pallas_call(
        functools.partial(_proj_kernel, prompt=prompt),
        out_shape=out_shape,
        grid=(nb, nt),
        in_specs=in_specs,
        out_specs=out_specs,
        scratch_shapes=scratch,
        compiler_params=pltpu.CompilerParams(dimension_semantics=("parallel", "arbitrary"),
                                             vmem_limit_bytes=VMEM_LIMIT),
        name="proj_prompt" if prompt else "proj_sample",
    )(*args)


def _reset(m_ref, l_ref, acc_ref):
    m_ref[...] = jnp.full(m_ref.shape, NEG, F32)
    l_ref[...] = jnp.zeros(l_ref.shape, F32)
    acc_ref[...] = jnp.zeros(acc_ref.shape, F32)


def _flash_block_t(st, vt, m_ref, l_ref, acc_ref):
    m_prev = m_ref[...]
    m_new = jnp.maximum(m_prev, jnp.max(st, axis=0, keepdims=True))
    alpha = jnp.exp2(m_prev - m_new)
    p = jnp.exp2(st - m_new)
    l_ref[...] = alpha * l_ref[...] + jnp.sum(p, axis=0, keepdims=True)
    acc_ref[...] = alpha * acc_ref[...] + _dot(vt, p.astype(BF16))
    m_ref[...] = m_new


def _causal_sweep(i, tq, block):
    block(0, True)

    def body(j, carry):
        block(j, False)
        return carry

    lax.fori_loop(1, i, body, 0)

    @pl.when(i > 0)
    def _():
        block(i, True)


def _key_mask(j, i, tq, nq, pad):
    krow = j * tq + lax.broadcasted_iota(jnp.int32, (tq, 1), 0)
    lane = lax.broadcasted_iota(jnp.int32, (1, nq), 1)
    qpos = i * tq + jnp.where(lane >= tq, lane - tq, lane)
    return (krow <= qpos) & (krow >= pad)


def _fox_prompt_kernel(qt_ref, ka_ref, vt_ref, o_ref, m_ref, l_ref, acc_ref, *, tq, pad):
    i = pl.program_id(1)
    hd = HEAD_DIM
    for kvh in range(FOX_KV_HEADS):
        h0, h1 = 2 * kvh, 2 * kvh + 1
        qt = jnp.concatenate([qt_ref[0, h0 * LANES:(h0 + 1) * LANES, :],
                              qt_ref[0, h1 * LANES:(h1 + 1) * LANES, :]], axis=1)
        _reset(m_ref, l_ref, acc_ref)

        def block(j, masked, kvh=kvh, qt=qt):
            ks = pl.multiple_of(j * tq, tq)
            st = _dot(ka_ref[0, pl.ds(ks, tq), kvh * LANES:(kvh + 1) * LANES], qt)
            if masked:
                st = jnp.where(_key_mask(j, i, tq, 2 * tq, pad), st, NEG)
            _flash_block_t(st, vt_ref[0, kvh * hd:(kvh + 1) * hd, pl.ds(ks, tq)], m_ref, l_ref, acc_ref)

        _causal_sweep(i, tq, block)
        o = acc_ref[...] / l_ref[...]
        o_ref[0, h0 * hd:(h0 + 1) * hd, :] = o[:, :tq]
        o_ref[0, h1 * hd:(h1 + 1) * hd, :] = o[:, tq:]


def _fox_prompt(qat, ka, vt, tq, pad):
    nb, _, lp = qat.shape
    return pl.pallas_call(
        functools.partial(_fox_prompt_kernel, tq=tq, pad=pad),
        out_shape=jax.ShapeDtypeStruct((nb, FOX_WIDTH, lp), F32),
        grid=(nb, lp // tq),
        in_specs=[pl.BlockSpec((1, FOX_HEADS * LANES, tq), lambda b, i: (b, 0, i)),
                  pl.BlockSpec((1, lp, FOX_KV_HEADS * LANES), lambda b, i: (b, 0, 0)),
                  pl.BlockSpec((1, FOX_KV_WIDTH, lp), lambda b, i: (b, 0, 0))],
        out_specs=pl.BlockSpec((1, FOX_WIDTH, tq), lambda b, i: (b, 0, i)),
        scratch_shapes=[pltpu.VMEM((1, 2 * tq), F32), pltpu.VMEM((1, 2 * tq), F32),
                        pltpu.VMEM((HEAD_DIM, 2 * tq), F32)],
        compiler_params=pltpu.CompilerParams(dimension_semantics=("parallel", "arbitrary"),
                                             vmem_limit_bytes=VMEM_LIMIT),
        name="fox_prompt",
    )(qat, ka, vt)


def _diff_lambda(dl_ref, lam_init):
    dl = dl_ref[...]
    a = jnp.sum(dl[0:1] * dl[1:2], axis=-1, keepdims=True)
    b = jnp.sum(dl[2:3] * dl[3:4], axis=-1, keepdims=True)
    return jnp.exp(a) - jnp.exp(b) + lam_init


def _diff_prompt_kernel(qt_ref, ka_ref, vt_ref, dl_ref, o_ref, m1_ref, l1_ref, a1_ref, m2_ref, l2_ref, a2_ref,
                        *, tq, pad, lam_init):
    i = pl.program_id(1)
    lam = _diff_lambda(dl_ref, lam_init)
    hd = HEAD_DIM
    for kvh in range(DIFF_KV_HEADS):
        r0 = kvh * 4 * hd
        q1 = jnp.concatenate([qt_ref[0, r0:r0 + hd, :], qt_ref[0, r0 + 2 * hd:r0 + 3 * hd, :]], axis=1)
        q2 = jnp.concatenate([qt_ref[0, r0 + hd:r0 + 2 * hd, :], qt_ref[0, r0 + 3 * hd:r0 + 4 * hd, :]], axis=1)
        _reset(m1_ref, l1_ref, a1_ref)
        _reset(m2_ref, l2_ref, a2_ref)

        def block(j, masked, kvh=kvh, q1=q1, q2=q2):
            ks = pl.multiple_of(j * tq, tq)
            c1, c2 = 2 * kvh * LANES, (2 * kvh + 1) * LANES
            s1 = _dot(ka_ref[0, pl.ds(ks, tq), c1:c1 + hd], q1)
            s2 = _dot(ka_ref[0, pl.ds(ks, tq), c2:c2 + hd], q2)
            if masked:
                mask = _key_mask(j, i, tq, 2 * tq, pad)
                s1, s2 = jnp.where(mask, s1, NEG), jnp.where(mask, s2, NEG)
            vt = vt_ref[0, kvh * 2 * hd:(kvh + 1) * 2 * hd, pl.ds(ks, tq)]
            _flash_block_t(s1, vt, m1_ref, l1_ref, a1_ref)
            _flash_block_t(s2, vt, m2_ref, l2_ref, a2_ref)

        _causal_sweep(i, tq, block)
        o = a1_ref[...] / l1_ref[...] - lam * a2_ref[...] / l2_ref[...]
        w = 2 * hd
        o_ref[0, (2 * kvh) * w:(2 * kvh + 1) * w, :] = o[:, :tq]
        o_ref[0, (2 * kvh + 1) * w:(2 * kvh + 2) * w, :] = o[:, tq:]


def _diff_prompt(dqt, dka, dvt, dlam, tq, pad, lam_init):
    nb, _, lp = dqt.shape
    stat = pltpu.VMEM((1, 2 * tq), F32)
    acc = pltpu.VMEM((2 * HEAD_DIM, 2 * tq), F32)
    return pl.pallas_call(
        functools.partial(_diff_prompt_kernel, tq=tq, pad=pad, lam_init=lam_init),
        out_shape=jax.ShapeDtypeStruct((nb, DIFF_WIDTH, lp), F32),
        grid=(nb, lp // tq),
        in_specs=[pl.BlockSpec((1, DIFF_WIDTH, tq), lambda b, i: (b, 0, i)),
                  pl.BlockSpec((1, lp, 2 * DIFF_KV_HEADS * LANES), lambda b, i: (b, 0, 0)),
                  pl.BlockSpec((1, DIFF_KV_WIDTH, lp), lambda b, i: (b, 0, 0)),
                  pl.BlockSpec((4, HEAD_DIM), lambda b, i: (0, 0))],
        out_specs=pl.BlockSpec((1, DIFF_WIDTH, tq), lambda b, i: (b, 0, i)),
        scratch_shapes=[stat, stat, acc, stat, stat, acc],
        compiler_params=pltpu.CompilerParams(dimension_semantics=("parallel", "arbitrary"),
                                             vmem_limit_bytes=VMEM_LIMIT),
        name="diff_prompt",
    )(dqt, dka, dvt, dlam)


def _pad_rows(x, rows):
    return jnp.concatenate([x, jnp.zeros((rows - x.shape[0], x.shape[1]), x.dtype)], axis=0)


def _page_pipeline(pt_ref, hbm_refs, bufs, sem, n_pages):
    b, nb = pl.program_id(0), pl.num_programs(0)

    def copies(seq, slot, pg):
        phys = pt_ref[seq * n_pages + pg]
        return [pltpu.make_async_copy(h.at[phys], buf.at[slot, pg], sem.at[k, slot])
                for k, (h, buf) in enumerate(zip(hbm_refs, bufs))]

    def for_pages(seq, slot, start):
        def body(pg, carry):
            for cp in copies(seq, slot, pg):
                cp.start() if start else cp.wait()
            return carry
        lax.fori_loop(0, n_pages, body, 0)

    @pl.when(b == 0)
    def _():
        for_pages(0, 0, True)

    @pl.when(b + 1 < nb)
    def _():
        for_pages(b + 1, (b + 1) % 2, True)

    slot = b % 2
    for_pages(b, slot, False)
    return slot


def _softmax_all(s_ref, p_ref):
    s3 = s_ref[...]
    m = jnp.max(jnp.max(s3, axis=0), axis=-1, keepdims=True)
    p3 = jnp.exp(s3 - m[None])
    p_ref[...] = p3.astype(BF16)
    return jnp.sum(jnp.sum(p3, axis=0), axis=-1, keepdims=True)


def _fox_sample_kernel(pt_ref, qbd_ref, lcol_ref, lrow_ref, knew_ref, vnew_ref, sup_ref, ck_hbm, cv_hbm, clf_hbm,
                       o_ref, kbuf, vbuf, lbuf, r_ref, s_ref, p_ref, sem, *, n_pages, t_new):
    slot = _page_pipeline(pt_ref, (ck_hbm, cv_hbm, clf_hbm), (kbuf, vbuf, lbuf), sem, n_pages)
    nh = FOX_HEADS
    rows = t_new * nh
    page = kbuf.shape[-1]
    qbd = qbd_ref[0]
    lc = lcol_ref[0]
    parts, run = [], None
    for t in range(t_new):
        run = lc[t * nh:(t + 1) * nh] if run is None else run + lc[t * nh:(t + 1) * nh]
        parts.append(run)
    qc = jnp.concatenate(parts, axis=0)

    lp_all = lbuf[slot].reshape(n_pages * nh, page)
    hi, mid, lo = _split3(lp_all)
    sup = sup_ref[...]
    r_in = _dot(hi, sup) + _dot(mid, sup) + _dot(lo, sup)
    tot = r_in[:, 0:1] + lp_all[:, 0:1]
    later, c = [], jnp.zeros((nh, 1), F32)
    for pg in reversed(range(n_pages)):
        later.append(c)
        c = c + tot[pg * nh:(pg + 1) * nh]
    r_ref[...] = r_in + jnp.concatenate(later[::-1], axis=0)

    def qk(pg, carry):
        r_pg = r_ref[pl.ds(pl.multiple_of(pg * nh, nh), nh), :]
        s_ref[pg] = _dot(qbd, kbuf[slot, pg].astype(BF16)) + jnp.concatenate([r_pg] * t_new, axis=0) + qc
        return carry

    lax.fori_loop(0, n_pages, qk, 0, unroll=PAGE_UNROLL)

    kn = _pad_rows(knew_ref[0], page).astype(BF16)
    vn = _pad_rows(vnew_ref[0], page).astype(BF16)
    lr = lrow_ref[0]
    lane8 = lax.broadcasted_iota(jnp.int32, lr.shape, 1)
    qct = jnp.zeros(lr.shape, F32)
    for t in range(t_new):
        qct = qct + jnp.where(lane8 >= t, lr[:, t:t + 1], 0.0)
    rr = lax.broadcasted_iota(jnp.int32, (rows, page), 0)
    cc = lax.broadcasted_iota(jnp.int32, (rows, page), 1)
    s_new = _dot_nt(qbd, kn) + qc - jnp.concatenate([qct] * t_new, axis=0)
    s_ref[n_pages] = jnp.where((cc * nh <= rr) & (cc < t_new), s_new, NEG)

    l = _softmax_all(s_ref, p_ref)

    def pv(pg, acc):
        return acc + _dot_nt(p_ref[pg], vbuf[slot, pg].astype(BF16))

    acc = lax.fori_loop(0, n_pages, pv, _dot(p_ref[n_pages], vn), unroll=PAGE_UNROLL)
    o = acc / l
    o_a, o_b = o[:, :LANES], o[:, LANES:]
    kv = (lax.broadcasted_iota(jnp.int32, (rows, LANES), 0) % nh) // (FOX_HEADS // FOX_KV_HEADS)
    sel = jnp.where(kv == 0, o_a, jnp.where(kv == 1, pltpu.roll(o_a, HEAD_DIM, 1),
                    jnp.where(kv == 2, o_b, pltpu.roll(o_b, HEAD_DIM, 1))))
    o_ref[0] = sel[:, :HEAD_DIM]


def _fox_sample(pt_flat, qbd, lcol, lrow, knew, vnew, ckt, cvt, clf, n_pages):
    db, rows, _ = qbd.shape
    t_new = rows // FOX_HEADS
    page = ckt.shape[-1]
    sup = jnp.asarray(np.tril(np.ones((page, page), np.float32), -1), dtype=BF16)
    per_b = lambda shape: pl.BlockSpec((1,) + shape, lambda b, pt: (b, 0, 0))
    hbm = pl.BlockSpec(memory_space=pl.ANY)
    in_specs = [per_b((rows, MXU_DIM)), per_b((rows, 1)), per_b((FOX_HEADS, LANES)),
                per_b((8, FOX_KV_WIDTH)), per_b((8, FOX_KV_WIDTH)),
                pl.BlockSpec((page, page), lambda b, pt: (0, 0)), hbm, hbm, hbm]
    return pl.pallas_call(
        functools.partial(_fox_sample_kernel, n_pages=n_pages, t_new=t_new),
        out_shape=jax.ShapeDtypeStruct((db, rows, HEAD_DIM), F32),
        grid_spec=pltpu.PrefetchScalarGridSpec(
            num_scalar_prefetch=1,
            grid=(db,),
            in_specs=in_specs,
            out_specs=pl.BlockSpec((1, rows, HEAD_DIM), lambda b, pt: (b, 0, 0)),
            scratch_shapes=[pltpu.VMEM((2, n_pages, FOX_KV_WIDTH, page), F32),
                            pltpu.VMEM((2, n_pages, FOX_KV_WIDTH, page), F32),
                            pltpu.VMEM((2, n_pages, FOX_HEADS, page), F32),
                            pltpu.VMEM((n_pages * FOX_HEADS, page), F32),
                            pltpu.VMEM((n_pages + 1, rows, page), F32),
                            pltpu.VMEM((n_pages + 1, rows, page), BF16),
                            pltpu.SemaphoreType.DMA((3, 2))]),
        compiler_params=pltpu.CompilerParams(dimension_semantics=("arbitrary",),
                                             vmem_limit_bytes=VMEM_LIMIT),
        name="fox_sample",
    )(pt_flat, qbd, lcol, lrow, knew, vnew, sup, ckt, cvt, clf)


def _diff_sample_kernel(pt_ref, qbd_ref, knew_ref, vnew_ref, dl_ref, ck_hbm, cv_hbm,
                        o_ref, kbuf, vbuf, s_ref, p_ref, sem, *, n_pages, t_new, lam_init):
    slot = _page_pipeline(pt_ref, (ck_hbm, cv_hbm), (kbuf, vbuf), sem, n_pages)
    half = t_new * DIFF_HEADS
    rows = 2 * half
    page = kbuf.shape[-1]
    qbd = qbd_ref[0]

    def qk(pg, carry):
        s_ref[pg] = _dot(qbd, kbuf[slot, pg].astype(BF16))
        return carry

    lax.fori_loop(0, n_pages, qk, 0, unroll=PAGE_UNROLL)

    kn = _pad_rows(knew_ref[0], page).astype(BF16)
    vn = _pad_rows(vnew_ref[0], page).astype(BF16)
    rr = lax.broadcasted_iota(jnp.int32, (rows, page), 0)
    cc = lax.broadcasted_iota(jnp.int32, (rows, page), 1)
    t_row = (rr % half) // DIFF_HEADS
    s_ref[n_pages] = jnp.where((cc <= t_row) & (cc < t_new), _dot_nt(qbd, kn), NEG)

    l = _softmax_all(s_ref, p_ref)

    def pv(pg, acc):
        a0, a1 = acc
        pp = p_ref[pg]
        v0 = vbuf[slot, pg, pl.ds(0, page, stride=DIFF_KV_HEADS), :].astype(BF16)
        v1 = vbuf[slot, pg, pl.ds(1, page, stride=DIFF_KV_HEADS), :].astype(BF16)
        return a0 + _dot(pp, v0), a1 + _dot(pp, v1)

    w = 2 * HEAD_DIM
    new = _dot(p_ref[n_pages], vn)
    a0, a1 = lax.fori_loop(0, n_pages, pv, (new[:, :w], new[:, w:]), unroll=PAGE_UNROLL)
    kv = (lax.broadcasted_iota(jnp.int32, (rows, w), 0) % DIFF_HEADS) // (DIFF_HEADS // DIFF_KV_HEADS)
    sel = jnp.where(kv == 0, a0, a1) / l
    o_ref[0] = sel[:half] - _diff_lambda(dl_ref, lam_init) * sel[half:]


def _diff_sample(pt_flat, qbd, knew, vnew, dlam, ckt, cv, n_pages, lam_init):
    db, rows, _ = qbd.shape
    t_new = rows // (2 * DIFF_HEADS)
    page = ckt.shape[-1]
    per_b = lambda shape: pl.BlockSpec((1,) + shape, lambda b, pt: (b, 0, 0))
    hbm = pl.BlockSpec(memory_space=pl.ANY)
    in_specs = [per_b((rows, MXU_DIM)), per_b((8, DIFF_KV_WIDTH)), per_b((8, DIFF_KV_WIDTH)),
                pl.BlockSpec((4, HEAD_DIM), lambda b, pt: (0, 0)), hbm, hbm]
    return pl.pallas_call(
        functools.partial(_diff_sample_kernel, n_pages=n_pages, t_new=t_new, lam_init=lam_init),
        out_shape=jax.ShapeDtypeStruct((db, rows // 2, 2 * HEAD_DIM), F32),
        grid_spec=pltpu.PrefetchScalarGridSpec(
            num_scalar_prefetch=1,
            grid=(db,),
            in_specs=in_specs,
            out_specs=pl.BlockSpec((1, rows // 2, 2 * HEAD_DIM), lambda b, pt: (b, 0, 0)),
            scratch_shapes=[pltpu.VMEM((2, n_pages, DIFF_KV_WIDTH, page), F32),
                            pltpu.VMEM((2, n_pages, DIFF_KV_HEADS * page, 2 * HEAD_DIM), F32),
                            pltpu.VMEM((n_pages + 1, rows, page), F32),
                            pltpu.VMEM((n_pages + 1, rows, page), BF16),
                            pltpu.SemaphoreType.DMA((2, 2))]),
        compiler_params=pltpu.CompilerParams(dimension_semantics=("arbitrary",),
                                             vmem_limit_bytes=VMEM_LIMIT),
        name="diff_sample",
    )(pt_flat, qbd, knew, vnew, dlam, ckt, cv)


def _heads_rmsnorm_t(xt, width, gain):
    parts = []
    for h in range(xt.shape[0] // width):
        xh = xt[h * width:(h + 1) * width]
        parts.append(xh * lax.rsqrt(jnp.mean(xh * xh, axis=0, keepdims=True) + EPS))
    return jnp.concatenate(parts, axis=0) * jnp.concatenate([gain] * (xt.shape[1] // LANES), axis=1)


def _ffn_kernel(*refs, tm, d_ff, post_scale, stateful):
    if stateful:
        (x_ref, fo_ref, do_ref, gm64_ref, gm128_ref, gfo_ref, gdo_ref, wo_ref, gffn_ref, wup_ref,
         cw_ref, cb_ref, wdn_ref, s0_ref, s1_ref, y_ref, h_ref) = refs
        fo = _group_rmsnorm(fo_ref[0], gm64_ref[...], gfo_ref[...])
        do = _group_rmsnorm(do_ref[0], gm128_ref[...], gdo_ref[...]) * post_scale
        merged = jnp.concatenate([fo, do], axis=1)
    else:
        (x_ref, fo_ref, do_ref, gfo_ref, gdo_ref, wo_ref, gffn_ref, wup_ref,
         cw_ref, cb_ref, wdn_ref, y_ref, h_ref, carry_ref) = refs

        @pl.when(pl.program_id(1) == 0)
        def _():
            carry_ref[...] = jnp.zeros_like(carry_ref)

        fo = _heads_rmsnorm_t(fo_ref[0], HEAD_DIM, gfo_ref[...])
        do = _heads_rmsnorm_t(do_ref[0], 2 * HEAD_DIM, gdo_ref[...]) * post_scale
        merged = jnp.concatenate([fo, do], axis=0).T

    att = _dot(merged.astype(BF16), wo_ref[...])
    x1 = x_ref[0] + att
    hn = x1 * lax.rsqrt(jnp.mean(x1 * x1, axis=-1, keepdims=True) + EPS) * gffn_ref[...]
    hnb = hn.astype(BF16)
    ch = FF_CHUNK
    rid = lax.broadcasted_iota(jnp.int32, (8, ch), 0)
    acc = x1
    for c in range(d_ff // ch):
        conv = []
        for half in range(2):
            col = half * d_ff + c * ch
            h = _dot(hnb, wup_ref[:, col:col + ch])
            if stateful:
                nseq = s0_ref.shape[0]
                hm1 = jnp.concatenate([s1_ref[:, col:col + ch], h[:tm - nseq]], axis=0)
                hm2 = jnp.concatenate([s0_ref[:, col:col + ch], s1_ref[:, col:col + ch], h[:tm - 2 * nseq]], axis=0)
                h_ref[:, col:col + ch] = h[tm - 2 * nseq:]
            else:
                prev = carry_ref[:, col:col + ch]
                r1, r2 = pltpu.roll(h, 1, 0), pltpu.roll(h, 2, 0)
                top1 = jnp.where(rid < 1, pltpu.roll(prev, 1, 0), r1[0:8])
                top2 = jnp.where(rid < 2, pltpu.roll(prev, 2, 0), r2[0:8])
                hm1 = jnp.concatenate([top1, r1[8:]], axis=0)
                hm2 = jnp.concatenate([top2, r2[8:]], axis=0)
                carry_ref[:, col:col + ch] = h[tm - 8:]
                h_ref[0, :, col:col + ch] = h[tm - 8:]
            conv.append(cb_ref[:, col:col + ch] + cw_ref[0:1, col:col + ch] * hm2
                        + cw_ref[1:2, col:col + ch] * hm1 + cw_ref[2:3, col:col + ch] * h)
        g, u = conv
        act = g * (1.0 / (1.0 + jnp.exp(-g))) * u
        acc = acc + _dot(act.astype(BF16), wdn_ref[c * ch:(c + 1) * ch, :])
    y_ref[0] = acc


def _merge_ffn(x, fo, do, p, tm, state=None):
    nb, lt, d = x.shape
    d_ff = p["w_down"].shape[0]
    stateful = state is not None
    single = pl.Buffered(1)
    const = lambda shape: pl.BlockSpec(shape, lambda b, i: (0,) * len(shape), pipeline_mode=single)
    weights = [const((FOX_WIDTH + DIFF_WIDTH, d)), const((1, d)), const((d, 2 * d_ff)), const((CONV_W, 2 * d_ff)),
               const((1, 2 * d_ff)), const((d_ff, d))]
    wargs = [p["w_o"], p["g_ffn"], p["w_up"], p["conv_w"], p["conv_b"], p["w_down"]]
    if stateful:
        nseq = state[0].shape[0]
        row = lambda w: pl.BlockSpec((1, tm, w), lambda b, i: (b, i, 0), pipeline_mode=single)
        in_specs = ([row(d), row(FOX_WIDTH), row(DIFF_WIDTH), const((MXU_DIM, MXU_DIM)), const((MXU_DIM, MXU_DIM)),
                     const((1, FOX_WIDTH)), const((1, DIFF_WIDTH))] + weights
                    + [const((nseq, 2 * d_ff)), const((nseq, 2 * d_ff))])
        args = [x, fo, do, p["gm64"], p["gm128"], p["g_fo"], p["g_do"]] + wargs + list(state)
        h_shape = jax.ShapeDtypeStruct((2 * nseq, 2 * d_ff), F32)
        h_spec = pl.BlockSpec((2 * nseq, 2 * d_ff), lambda b, i: (0, 0))
        scratch = []
    else:
        in_specs = ([pl.BlockSpec((1, tm, d), lambda b, i: (b, i, 0)),
                     pl.BlockSpec((1, FOX_WIDTH, tm), lambda b, i: (b, 0, i)),
                     pl.BlockSpec((1, DIFF_WIDTH, tm), lambda b, i: (b, 0, i)),
                     const((FOX_WIDTH, LANES)), const((DIFF_WIDTH, LANES))] + weights)
        args = [x, fo, do, p["g_fo_t"], p["g_do_t"]] + wargs
        h_shape = jax.ShapeDtypeStruct((nb, 8, 2 * d_ff), F32)
        h_spec = pl.BlockSpec((1, 8, 2 * d_ff), lambda b, i: (b, 0, 0))
        scratch = [pltpu.VMEM((8, 2 * d_ff), F32)]
    kern = functools.partial(_ffn_kernel, tm=tm, d_ff=d_ff, post_scale=p["post_scale"], stateful=stateful)
    return pl.pallas_call(
        kern,
        out_shape=[jax.ShapeDtypeStruct((nb, lt, d), F32), h_shape],
        grid=(nb, lt // tm),
        in_specs=in_specs,
        out_specs=[pl.BlockSpec((1, tm, d), lambda b, i: (b, i, 0)), h_spec],
        scratch_shapes=scratch,
        compiler_params=pltpu.CompilerParams(dimension_semantics=("parallel", "arbitrary"),
                                             vmem_limit_bytes=VMEM_LIMIT),
        name="merge_ffn_sample" if stateful else "merge_ffn_prompt",
    )(*args)


def kernel(x_prompt, x_sample, cache_fox_k, cache_fox_v, cache_fox_logf, cache_diff_k, cache_diff_v, state_ffn_conv, page_table, meta_tokens, attn_norm_g, w_in, b_f, fox_qn_g, fox_kn_g, fox_on_g, diff_qn_g, diff_kn_g, diff_lambda, diff_subln_g, w_o, ffn_norm_g, w_up, conv_w, conv_b, w_down):
    nb, seq, d = x_prompt.shape
    db, t_new, _ = x_sample.shape
    depth, n_phys, page = cache_fox_k.shape[:3]
    n_pages = page_table.shape[1]
    assert depth == 1, "single layer only"
    layer = 0
    lam_init = 0.8 - 0.6 * math.exp(-0.3 * layer)
    past_len = n_pages * page

    wi = w_in[layer]
    o = np.cumsum([0, FOX_WIDTH, FOX_KV_WIDTH, FOX_KV_WIDTH, FOX_HEADS, DIFF_WIDTH, DIFF_KV_WIDTH, DIFF_KV_WIDTH])
    seg = lambda k: wi[:, o[k]:o[k + 1]]
    w_all = jnp.concatenate([seg(0), seg(1), seg(2), seg(4), seg(5), seg(6),
                             jnp.pad(seg(3), ((0, 0), (0, LANES - FOX_HEADS)))], axis=1).astype(BF16)
    g_fo = fox_on_g[layer].reshape(1, FOX_WIDTH)
    g_do = jnp.tile(diff_subln_g[layer], DIFF_HEADS)[None, :]
    lane_rep = lambda g: jnp.broadcast_to(g.reshape(-1, 1), (g.shape[-1], LANES))
    p = {
        "g_attn": attn_norm_g[layer][None, :],
        "w_in": w_all,
        "gm64": _block_diag_mean(HEAD_DIM),
        "gm128": _block_diag_mean(2 * HEAD_DIM),
        "g_heads": jnp.concatenate([jnp.tile(fox_qn_g[layer], FOX_HEADS), jnp.tile(fox_kn_g[layer], FOX_KV_HEADS),
                                    jnp.tile(diff_qn_g[layer], 2 * DIFF_HEADS),
                                    jnp.tile(diff_kn_g[layer], 2 * DIFF_KV_HEADS)])[None, :],
        "b_f": jnp.pad(b_f[layer], (0, LANES - FOX_HEADS))[None, :],
        "g_fo": g_fo, "g_do": g_do, "g_fo_t": lane_rep(g_fo), "g_do_t": lane_rep(g_do),
        "post_scale": 1.0 - lam_init,
        "w_o": w_o[layer].astype(BF16),
        "g_ffn": ffn_norm_g[layer][None, :],
        "w_up": w_up[layer].astype(BF16),
        "conv_w": conv_w[layer],
        "conv_b": conv_b[layer][None, :],
        "w_down": w_down[layer].astype(BF16),
    }
    dlam = diff_lambda[layer]

    tm = ROW_TILE
    pad = (-N_META) % tm
    lp = pad + N_META + seq
    xp = jnp.concatenate([jnp.zeros((nb, pad, d), x_prompt.dtype),
                          jnp.broadcast_to(meta_tokens[None].astype(x_prompt.dtype), (nb, N_META, d)),
                          x_prompt], axis=1)
    pos_p = jnp.arange(lp) - pad
    fk, fv, lf, dk, dv, qat, ka, vt, dqt, dka, dvt = _project(xp, pos_p, tm, p, True)
    fo_t = _fox_prompt(qat, ka, vt, tm, pad)
    do_t = _diff_prompt(dqt, dka, dvt, dlam, tm, pad, lam_init)
    yp, hlast = _merge_ffn(xp, fo_t, do_t, p, tm)
    ltot = N_META + seq
    y_prompt = yp[:, pad + N_META:]
    pk = fk[:, pad:].reshape(1, nb, ltot, FOX_KV_HEADS, HEAD_DIM)
    pv = fv[:, pad:].reshape(1, nb, ltot, FOX_KV_HEADS, HEAD_DIM)
    pf = lf[:, pad:][None]
    pdk = dk[:, pad:].reshape(1, nb, ltot, DIFF_KV_HEADS, 2, HEAD_DIM)
    pdv = dv[:, pad:].reshape(1, nb, ltot, DIFF_KV_HEADS, 2 * HEAD_DIM)
    pc = hlast[:, 8 - (CONV_W - 1):][None]

    rows = db * t_new
    pos_s = past_len + (jnp.arange(rows) % t_new)
    qf, fk, fv, lf, dqb, dk, dv = _project(x_sample.reshape(1, rows, d), pos_s, min(ROW_TILE, rows), p, False)
    g_f = FOX_HEADS // FOX_KV_HEADS
    g_d = DIFF_HEADS // DIFF_KV_HEADS
    eye_f = jnp.eye(FOX_KV_HEADS, dtype=BF16)
    qbd_f = jnp.einsum('btkgd,kj->btkgjd', qf.reshape(db, t_new, FOX_KV_HEADS, g_f, HEAD_DIM), eye_f)
    qbd_f = qbd_f.reshape(db, t_new * FOX_HEADS, FOX_KV_WIDTH)
    eye_d, eye_2 = jnp.eye(DIFF_KV_HEADS, dtype=BF16), jnp.eye(2, dtype=BF16)
    qbd_d = jnp.einsum('btkgwd,kj,wv->bwtkgjvd', dqb.reshape(db, t_new, DIFF_KV_HEADS, g_d, 2, HEAD_DIM), eye_d, eye_2)
    qbd_d = qbd_d.reshape(db, 2 * t_new * DIFF_HEADS, DIFF_KV_WIDTH)
    lf_s = lf.reshape(db, t_new, FOX_HEADS)
    lcol = lf_s.reshape(db, t_new * FOX_HEADS, 1)
    lrow = jnp.pad(jnp.swapaxes(lf_s, 1, 2), ((0, 0), (0, 0), (0, LANES - t_new)))
    new_rows = lambda a: jnp.pad(a.reshape(db, t_new, a.shape[-1]), ((0, 0), (0, 8 - t_new), (0, 0)))
    pt_flat = page_table.reshape(-1).astype(jnp.int32)
    ckt = jnp.transpose(cache_fox_k[layer], (0, 2, 3, 1)).reshape(n_phys, FOX_KV_WIDTH, page)
    cvt = jnp.transpose(cache_fox_v[layer], (0, 2, 3, 1)).reshape(n_phys, FOX_KV_WIDTH, page)
    clf = jnp.swapaxes(cache_fox_logf[layer], 1, 2)
    fo_s = _fox_sample(pt_flat, qbd_f, lcol, lrow, new_rows(fk), new_rows(fv), ckt, cvt, clf, n_pages)
    cdkt = jnp.transpose(cache_diff_k[layer], (0, 2, 3, 4, 1)).reshape(n_phys, DIFF_KV_WIDTH, page)
    cdv = cache_diff_v[layer].reshape(n_phys, page * DIFF_KV_HEADS, 2 * HEAD_DIM)
    do_s = _diff_sample(pt_flat, qbd_d, new_rows(dk), new_rows(dv), dlam, cdkt, cdv, n_pages, lam_init)

    tmajor = lambda a, w: jnp.swapaxes(a.reshape(db, t_new, w), 0, 1).reshape(1, rows, w)
    st = state_ffn_conv[layer]
    ys, hs = _merge_ffn(tmajor(x_sample, d), tmajor(fo_s, FOX_WIDTH), tmajor(do_s, DIFF_WIDTH), p, rows,
                        state=(st[:, 0], st[:, 1]))
    y_sample = jnp.swapaxes(ys.reshape(t_new, db, d), 0, 1)
    sc = jnp.swapaxes(hs.reshape(CONV_W - 1, db, hs.shape[-1]), 0, 1)[None]
    sk = fk.reshape(1, db, t_new, FOX_KV_HEADS, HEAD_DIM)
    sv = fv.reshape(1, db, t_new, FOX_KV_HEADS, HEAD_DIM)
    sf = lf_s[None]
    sdk = dk.reshape(1, db, t_new, DIFF_KV_HEADS, 2, HEAD_DIM)
    sdv = dv.reshape(1, db, t_new, DIFF_KV_HEADS, 2 * HEAD_DIM)
    return (y_prompt, y_sample, pk, pv, pf, pdk, pdv, pc, sk, sv, sf, sdk, sdv, sc)
```

```python
import functools
import math

import numpy as np
import jax
import jax.numpy as jnp
from jax import lax
from jax.experimental import pallas as pl
from jax.experimental.pallas import tpu as pltpu

F32 = jnp.float32
BF16 = jnp.bfloat16

HEAD_DIM = 64
FOX_HEADS = 8
FOX_KV_HEADS = 4
DIFF_HEADS = 4
DIFF_KV_HEADS = 2
FOX_WIDTH = FOX_HEADS * HEAD_DIM
FOX_KV_WIDTH = FOX_KV_HEADS * HEAD_DIM
DIFF_WIDTH = DIFF_HEADS * 2 * HEAD_DIM
DIFF_KV_WIDTH = DIFF_KV_HEADS * 2 * HEAD_DIM
ROT_DIM = HEAD_DIM // 4
ROPE_THETA = 500000.0
CONV_W = 3
N_META = 16
EPS = 1e-6
NEG = -1e30
SCALE = HEAD_DIM ** -0.5
LOG2E = 1.4426950408889634

LANES = 128
MXU_DIM = 256
ROW_TILE = 256
FF_CHUNK = 256
PAGE_UNROLL = 16
VMEM_LIMIT = 56 * 1024 * 1024

_FQ, _FK, _FV, _DQ, _DK, _DV, _FF, _W_TOTAL = 0, 512, 768, 1024, 1536, 1792, 2048, 2176
_AUG_C, _AUG_G0 = HEAD_DIM, HEAD_DIM + 3


def _split3(x):
    hi = x.astype(BF16)
    r = x - hi.astype(F32)
    mid = r.astype(BF16)
    lo = (r - mid.astype(F32)).astype(BF16)
    return hi, mid, lo


def _dot(a, b):
    return jnp.dot(a, b, preferred_element_type=F32)


def _dot_nt(a, b):
    return lax.dot_general(a, b, (((1,), (1,)), ((), ())), preferred_element_type=F32)


def _group_mean_sq(x, gmat):
    outs = []
    for c in range(x.shape[1] // MXU_DIM):
        s = x[:, c * MXU_DIM:(c + 1) * MXU_DIM]
        outs.append(_dot((s * s).astype(BF16), gmat))
    return outs[0] if len(outs) == 1 else jnp.concatenate(outs, axis=1)


def _group_rmsnorm(x, gmat, gain):
    return x * lax.rsqrt(_group_mean_sq(x, gmat) + EPS) * gain


def _proj_kernel(*refs, prompt):
    (x_ref, gattn_ref, w_ref, gm_ref, gh_ref, bf_ref, cos_ref, sa_ref, sb_ref) = refs[:9]
    if prompt:
        (tri_ref, pq_ref, pk_ref, oq_ref, ok_ref, head_ref,
         fk_ref, fv_ref, lf_ref, dk_ref, dv_ref,
         qat_ref, ka_ref, vt_ref, dqt_ref, dka_ref, dvt_ref, carry_ref) = refs[9:]
    else:
        (qf_ref, fk_ref, fv_ref, lf_ref, dqb_ref, dk_ref, dv_ref) = refs[9:]
    x = x_ref[0]
    if prompt:
        x = jnp.where(pl.program_id(1) == 0, head_ref[...], x)
    xn = x * lax.rsqrt(jnp.mean(x * x, axis=-1, keepdims=True) + EPS) * gattn_ref[...]
    y = _dot(xn.astype(BF16), w_ref[...])
    gm = gm_ref[...]
    cos_t, sin_a, sin_b = cos_ref[...], sa_ref[...], sb_ref[...]

    def rope(seg):
        outs = []
        for c in range(seg.shape[1] // LANES):
            s = seg[:, c * LANES:(c + 1) * LANES]
            outs.append(s * cos_t + pltpu.roll(s, LANES - ROT_DIM // 2, 1) * sin_a
                        + pltpu.roll(s, ROT_DIM // 2, 1) * sin_b)
        return jnp.concatenate(outs, axis=1)

    fq = _group_rmsnorm(y[:, _FQ:_FK], gm, gh_ref[:, 0:512])
    fk = _group_rmsnorm(y[:, _FK:_FV], gm, gh_ref[:, 512:768])
    fv = y[:, _FV:_DQ]
    dq = rope(_group_rmsnorm(y[:, _DQ:_DK], gm, gh_ref[:, 768:1280]))
    dk = rope(_group_rmsnorm(y[:, _DK:_DV], gm, gh_ref[:, 1280:1536]))
    dv = y[:, _DV:_FF]
    z = y[:, _FF:_W_TOTAL] + bf_ref[...]
    lf = jnp.minimum(z, 0.0) - jnp.log1p(jnp.exp(-jnp.abs(z)))
    fk_ref[0] = fk
    fv_ref[0] = fv
    dk_ref[0] = dk
    dv_ref[0] = dv
    lf_ref[0] = lf[:, :FOX_HEADS]
    if not prompt:
        qf_ref[0] = (fq * SCALE).astype(BF16)
        dqb_ref[0] = (dq * SCALE).astype(BF16)
        return

    lane = lax.broadcasted_iota(jnp.int32, lf.shape, 1)
    lf = jnp.where(lane < FOX_HEADS, lf, 0.0)
    hi, mid, lo = _split3(lf)
    tri = tri_ref[...]
    cs = _dot(tri, hi) + _dot(tri, mid) + _dot(tri, lo)

    @pl.when(pl.program_id(1) == 0)
    def _():
        carry_ref[...] = jnp.zeros_like(carry_ref)

    c = cs + carry_ref[0:1, :]
    tm = c.shape[0]
    carry_ref[0:1, :] = c[tm - 1:tm, :]

    x3 = jnp.concatenate(_split3(c * LOG2E), axis=1)
    eq = _dot(x3, pq_ref[...]) + oq_ref[...]
    ek = _dot(x3, pk_ref[...]) + ok_ref[...]
    low = lane < HEAD_DIM
    half = lambda chunk, odd: pltpu.roll(chunk, HEAD_DIM, 1) if odd else chunk
    fqs = fq * (SCALE * LOG2E)
    for h in range(FOX_HEADS):
        chunk = fqs[:, (h // 2) * LANES:(h // 2 + 1) * LANES]
        qa = jnp.where(low, half(chunk, h % 2), eq[:, h * LANES:(h + 1) * LANES])
        qat_ref[0, h * LANES:(h + 1) * LANES, :] = qa.T.astype(BF16)
    for h in range(FOX_KV_HEADS):
        chunk = fk[:, (h // 2) * LANES:(h // 2 + 1) * LANES]
        ka = jnp.where(low, half(chunk, h % 2), ek[:, h * LANES:(h + 1) * LANES])
        ka_ref[0, :, h * LANES:(h + 1) * LANES] = ka.astype(BF16)
    for c2 in range(FOX_KV_WIDTH // LANES):
        vt_ref[0, c2 * LANES:(c2 + 1) * LANES, :] = fv[:, c2 * LANES:(c2 + 1) * LANES].T.astype(BF16)
    dqs = dq * (SCALE * LOG2E)
    for c2 in range(DIFF_WIDTH // LANES):
        dqt_ref[0, c2 * LANES:(c2 + 1) * LANES, :] = dqs[:, c2 * LANES:(c2 + 1) * LANES].T.astype(BF16)
    for h in range(2 * DIFF_KV_HEADS):
        chunk = dk[:, (h // 2) * LANES:(h // 2 + 1) * LANES]
        dka_ref[0, :, h * LANES:(h + 1) * LANES] = jnp.where(low, half(chunk, h % 2), 0.0).astype(BF16)
    for c2 in range(DIFF_KV_WIDTH // LANES):
        dvt_ref[0, c2 * LANES:(c2 + 1) * LANES, :] = dv[:, c2 * LANES:(c2 + 1) * LANES].T.astype(BF16)


def _rope_tables(pos):
    inv_freq = ROPE_THETA ** (-jnp.arange(0, ROT_DIM, 2, dtype=F32) / ROT_DIM)
    ang = pos.astype(F32)[:, None] * inv_freq[None, :]
    cos, sin = jnp.cos(ang), jnp.sin(ang)
    half = ROT_DIM // 2
    ones = jnp.ones((pos.shape[0], HEAD_DIM - ROT_DIM), F32)
    zeros_h = jnp.zeros((pos.shape[0], half), F32)
    zeros_r = jnp.zeros((pos.shape[0], HEAD_DIM - ROT_DIM), F32)
    cos_h = jnp.concatenate([cos, cos, ones], axis=1)
    sa_h = jnp.concatenate([-sin, zeros_h, zeros_r], axis=1)
    sb_h = jnp.concatenate([zeros_h, sin, zeros_r], axis=1)
    rep = LANES // HEAD_DIM
    return jnp.tile(cos_h, (1, rep)), jnp.tile(sa_h, (1, rep)), jnp.tile(sb_h, (1, rep))


def _block_diag_mean(group):
    idx = np.arange(MXU_DIM) // group
    return jnp.asarray((idx[:, None] == idx[None, :]).astype(np.float32) / group, dtype=BF16)


def _bias_placement():
    g = FOX_HEADS // FOX_KV_HEADS
    pq = np.zeros((3 * LANES, FOX_HEADS * LANES), np.float32)
    pk = np.zeros((3 * LANES, FOX_KV_HEADS * LANES), np.float32)
    oq = np.zeros((1, FOX_HEADS * LANES), np.float32)
    ok = np.zeros((1, FOX_KV_HEADS * LANES), np.float32)
    for h in range(FOX_HEADS):
        kvh, gi = divmod(h, g)
        for piece in range(3):
            pq[piece * LANES + h, h * LANES + _AUG_C + piece] = 1.0
            pk[piece * LANES + h, kvh * LANES + _AUG_G0 + 3 * gi + piece] = -1.0
            oq[0, h * LANES + _AUG_G0 + 3 * gi + piece] = 1.0
            ok[0, kvh * LANES + _AUG_C + piece] = 1.0
    return jnp.asarray(pq, BF16), jnp.asarray(pk, BF16), jnp.asarray(oq), jnp.asarray(ok)


def _project(x, pos, tm, p, head=None):
    prompt = head is not None
    nb, lt, d = x.shape
    lt += tm if prompt else 0
    nt = lt // tm
    cos_t, sin_a, sin_b = _rope_tables(pos)
    row = lambda w, dt=None: pl.BlockSpec((1, tm, w), lambda b, i: (b, i, 0))
    col = lambda w: pl.BlockSpec((1, w, tm), lambda b, i: (b, 0, i))
    const = lambda shape: pl.BlockSpec(shape, lambda b, i: (0,) * len(shape))
    tab = pl.BlockSpec((tm, LANES), lambda b, i: (i, 0))
    sds = jax.ShapeDtypeStruct
    x_spec = pl.BlockSpec((1, tm, d), lambda b, i: (b, jnp.maximum(i - 1, 0), 0)) if prompt else row(d)
    in_specs = [x_spec, const((1, d)), const((d, _W_TOTAL)), const((MXU_DIM, MXU_DIM)),
                const((1, 1536)), const((1, LANES)), tab, tab, tab]
    args = [x, p["g_attn"], p["w_in"], p["gm64"], p["g_heads"], p["b_f"], cos_t, sin_a, sin_b]
    f32_rows = [sds((nb, lt, FOX_KV_WIDTH), F32), sds((nb, lt, FOX_KV_WIDTH), F32), sds((nb, lt, FOX_HEADS), F32),
                sds((nb, lt, DIFF_KV_WIDTH), F32), sds((nb, lt, DIFF_KV_WIDTH), F32)]
    f32_specs = [row(FOX_KV_WIDTH), row(FOX_KV_WIDTH), row(FOX_HEADS), row(DIFF_KV_WIDTH), row(DIFF_KV_WIDTH)]
    if prompt:
        tri = jnp.asarray(np.tril(np.ones((tm, tm), np.float32)), dtype=BF16)
        pq, pk, oq, ok = _bias_placement()
        in_specs += [const((tm, tm)), const(pq.shape), const(pk.shape), const(oq.shape), const(ok.shape),
                     const((tm, d))]
        args += [tri, pq, pk, oq, ok, head]
        out_shape = f32_rows + [sds((nb, FOX_HEADS * LANES, lt), BF16), sds((nb, lt, FOX_KV_HEADS * LANES), BF16),
                                sds((nb, FOX_KV_WIDTH, lt), BF16), sds((nb, DIFF_WIDTH, lt), BF16),
                                sds((nb, lt, 2 * DIFF_KV_HEADS * LANES), BF16), sds((nb, DIFF_KV_WIDTH, lt), BF16)]
        out_specs = f32_specs + [col(FOX_HEADS * LANES), row(FOX_KV_HEADS * LANES), col(FOX_KV_WIDTH),
                                 col(DIFF_WIDTH), row(2 * DIFF_KV_HEADS * LANES), col(DIFF_KV_WIDTH)]
        scratch = [pltpu.VMEM((8, LANES), F32)]
    else:
        out_shape = ([sds((nb, lt, FOX_WIDTH), BF16)] + f32_rows[:3] + [sds((nb, lt, DIFF_WIDTH), BF16)] + f32_rows[3:])
        out_specs = [row(FOX_WIDTH)] + f32_specs[:3] + [row(DIFF_WIDTH)] + f32_specs[3:]
        scratch = []
    return pl.pallas_call(
        functools.partial(_proj_kernel, prompt=prompt),
        out_shape=out_shape,
        grid=(nb, nt),
        in_specs=in_specs,
        out_specs=out_specs,
        scratch_shapes=scratch,
        compiler_params=pltpu.CompilerParams(dimension_semantics=("parallel", "arbitrary"),
                                             vmem_limit_bytes=VMEM_LIMIT),
        name="proj_prompt" if prompt else "proj_sample",
    )(*args)


def _reset(m_ref, l_ref, acc_ref):
    m_ref[...] = jnp.full(m_ref.shape, NEG, F32)
    l_ref[...] = jnp.zeros(l_ref.shape, F32)
    acc_ref[...] = jnp.zeros(acc_ref.shape, F32)


def _flash_block_t(st, vt, m_ref, l_ref, acc_ref):
    m_prev = m_ref[...]
    m_new = jnp.maximum(m_prev, jnp.max(st, axis=0, keepdims=True))
    alpha = jnp.exp2(m_prev - m_new)
    p = jnp.exp2(st - m_new)
    l_ref[...] = alpha * l_ref[...] + jnp.sum(p, axis=0, keepdims=True)
    acc_ref[...] = alpha * acc_ref[...] + _dot(vt, p.astype(BF16))
    m_ref[...] = m_new


def _causal_sweep(i, produce, consume):
    produce(0, 0)

    @pl.when(i == 0)
    def _():
        consume(0, 0, True)

    @pl.when(i > 0)
    def _():
        produce(1, 1)
        consume(0, 0, True)

        def body(jj, carry):
            j = 2 * jj + 1
            produce(j + 1, 0)
            consume(j, 1, False)

            @pl.when(j + 1 < i)
            def _():
                produce(j + 2, 1)
                consume(j + 1, 0, False)
            return carry

        lax.fori_loop(0, i // 2, body, 0)

        @pl.when(i % 2 == 1)
        def _():
            consume(i, 1, True)

        @pl.when(i % 2 == 0)
        def _():
            consume(i, 0, True)


def _key_mask(j, i, tq, nq, pad):
    krow = j * tq + lax.broadcasted_iota(jnp.int32, (tq, 1), 0)
    lane = lax.broadcasted_iota(jnp.int32, (1, nq), 1)
    qpos = i * tq + jnp.where(lane >= tq, lane - tq, lane)
    return (krow <= qpos) & (krow >= pad)


def _fox_prompt_kernel(qt_ref, ka_ref, vt_ref, o_ref, m_ref, l_ref, acc_ref, s_ref, *, tq, pad):
    i = pl.program_id(1)
    hd = HEAD_DIM
    _reset(m_ref, l_ref, acc_ref)

    def produce(j, slot):
        ks = pl.multiple_of(j * tq, tq)
        for kvh in range(FOX_KV_HEADS):
            h0, h1 = 2 * kvh, 2 * kvh + 1
            qt = jnp.concatenate([qt_ref[0, h0 * LANES:(h0 + 1) * LANES, :],
                                  qt_ref[0, h1 * LANES:(h1 + 1) * LANES, :]], axis=1)
            s_ref[slot, kvh] = _dot(ka_ref[0, pl.ds(ks, tq), kvh * LANES:(kvh + 1) * LANES], qt)

    def consume(j, slot, masked):
        ks = pl.multiple_of(j * tq, tq)
        mask = _key_mask(j, i, tq, 2 * tq, pad) if masked else None
        for kvh in range(FOX_KV_HEADS):
            st = s_ref[slot, kvh]
            if masked:
                st = jnp.where(mask, st, NEG)
            _flash_block_t(st, vt_ref[0, kvh * hd:(kvh + 1) * hd, pl.ds(ks, tq)],
                           m_ref.at[kvh], l_ref.at[kvh], acc_ref.at[kvh])

    _causal_sweep(i, produce, consume)
    for kvh in range(FOX_KV_HEADS):
        h0, h1 = 2 * kvh, 2 * kvh + 1
        o = acc_ref[kvh] / l_ref[kvh]
        o_ref[0, h0 * hd:(h0 + 1) * hd, :] = o[:, :tq]
        o_ref[0, h1 * hd:(h1 + 1) * hd, :] = o[:, tq:]


def _fox_prompt(qat, ka, vt, tq, pad):
    nb, _, lp = qat.shape
    return pl.pallas_call(
        functools.partial(_fox_prompt_kernel, tq=tq, pad=pad),
        out_shape=jax.ShapeDtypeStruct((nb, FOX_WIDTH, lp), F32),
        grid=(nb, lp // tq),
        in_specs=[pl.BlockSpec((1, FOX_HEADS * LANES, tq), lambda b, i: (b, 0, i)),
                  pl.BlockSpec((1, lp, FOX_KV_HEADS * LANES), lambda b, i: (b, 0, 0)),
                  pl.BlockSpec((1, FOX_KV_WIDTH, lp), lambda b, i: (b, 0, 0))],
        out_specs=pl.BlockSpec((1, FOX_WIDTH, tq), lambda b, i: (b, 0, i)),
        scratch_shapes=[pltpu.VMEM((FOX_KV_HEADS, 1, 2 * tq), F32), pltpu.VMEM((FOX_KV_HEADS, 1, 2 * tq), F32),
                        pltpu.VMEM((FOX_KV_HEADS, HEAD_DIM, 2 * tq), F32),
                        pltpu.VMEM((2, FOX_KV_HEADS, tq, 2 * tq), F32)],
        compiler_params=pltpu.CompilerParams(dimension_semantics=("parallel", "arbitrary"),
                                             vmem_limit_bytes=VMEM_LIMIT),
        name="fox_prompt",
    )(qat, ka, vt)


def _diff_lambda(dl_ref, lam_init):
    dl = dl_ref[...]
    a = jnp.sum(dl[0:1] * dl[1:2], axis=-1, keepdims=True)
    b = jnp.sum(dl[2:3] * dl[3:4], axis=-1, keepdims=True)
    return jnp.exp(a) - jnp.exp(b) + lam_init


def _diff_prompt_kernel(qt_ref, ka_ref, vt_ref, dl_ref, o_ref, m_ref, l_ref, acc_ref, s_ref, *, tq, pad, lam_init):
    i = pl.program_id(1)
    hd = HEAD_DIM
    _reset(m_ref, l_ref, acc_ref)
    streams = [(kvh, w) for kvh in range(DIFF_KV_HEADS) for w in range(2)]

    def produce(j, slot):
        ks = pl.multiple_of(j * tq, tq)
        for n, (kvh, w) in enumerate(streams):
            r0 = (kvh * 4 + w) * hd
            qt = jnp.concatenate([qt_ref[0, r0:r0 + hd, :], qt_ref[0, r0 + 2 * hd:r0 + 3 * hd, :]], axis=1)
            s_ref[slot, n] = _dot(ka_ref[0, pl.ds(ks, tq), n * LANES:n * LANES + hd], qt)

    def consume(j, slot, masked):
        ks = pl.multiple_of(j * tq, tq)
        mask = _key_mask(j, i, tq, 2 * tq, pad) if masked else None
        for n, (kvh, w) in enumerate(streams):
            st = s_ref[slot, n]
            if masked:
                st = jnp.where(mask, st, NEG)
            _flash_block_t(st, vt_ref[0, kvh * 2 * hd:(kvh + 1) * 2 * hd, pl.ds(ks, tq)],
                           m_ref.at[n], l_ref.at[n], acc_ref.at[n])

    _causal_sweep(i, produce, consume)
    lam = _diff_lambda(dl_ref, lam_init)
    w = 2 * hd
    for kvh in range(DIFF_KV_HEADS):
        n1, n2 = 2 * kvh, 2 * kvh + 1
        o = acc_ref[n1] / l_ref[n1] - lam * acc_ref[n2] / l_ref[n2]
        o_ref[0, (2 * kvh) * w:(2 * kvh + 1) * w, :] = o[:, :tq]
        o_ref[0, (2 * kvh + 1) * w:(2 * kvh + 2) * w, :] = o[:, tq:]


def _diff_prompt(dqt, dka, dvt, dlam, tq, pad, lam_init):
    nb, _, lp = dqt.shape
    ns = 2 * DIFF_KV_HEADS
    return pl.pallas_call(
        functools.partial(_diff_prompt_kernel, tq=tq, pad=pad, lam_init=lam_init),
        out_shape=jax.ShapeDtypeStruct((nb, DIFF_WIDTH, lp), F32),
        grid=(nb, lp // tq),
        in_specs=[pl.BlockSpec((1, DIFF_WIDTH, tq), lambda b, i: (b, 0, i)),
                  pl.BlockSpec((1, lp, 2 * DIFF_KV_HEADS * LANES), lambda b, i: (b, 0, 0)),
                  pl.BlockSpec((1, DIFF_KV_WIDTH, lp), lambda b, i: (b, 0, 0)),
                  pl.BlockSpec((4, HEAD_DIM), lambda b, i: (0, 0))],
        out_specs=pl.BlockSpec((1, DIFF_WIDTH, tq), lambda b, i: (b, 0, i)),
        scratch_shapes=[pltpu.VMEM((ns, 1, 2 * tq), F32), pltpu.VMEM((ns, 1, 2 * tq), F32),
                        pltpu.VMEM((ns, 2 * HEAD_DIM, 2 * tq), F32),
                        pltpu.VMEM((2, ns, tq, 2 * tq), F32)],
        compiler_params=pltpu.CompilerParams(dimension_semantics=("parallel", "arbitrary"),
                                             vmem_limit_bytes=VMEM_LIMIT),
        name="diff_prompt",
    )(dqt, dka, dvt, dlam)


def _pad_rows(x, rows):
    return jnp.concatenate([x, jnp.zeros((rows - x.shape[0], x.shape[1]), x.dtype)], axis=0)


def _page_pipeline(pt_ref, hbm_refs, bufs, sem, n_pages):
    b, nb = pl.program_id(0), pl.num_programs(0)

    def copies(seq, slot, pg):
        phys = pt_ref[seq * n_pages + pg]
        return [pltpu.make_async_copy(h.at[phys], buf.at[slot, pg], sem.at[k, slot])
                for k, (h, buf) in enumerate(zip(hbm_refs, bufs))]

    def for_pages(seq, slot, start):
        def body(pg, carry):
            for cp in copies(seq, slot, pg):
                cp.start() if start else cp.wait()
            return carry
        lax.fori_loop(0, n_pages, body, 0)

    @pl.when(b == 0)
    def _():
        for_pages(0, 0, True)

    @pl.when(b + 1 < nb)
    def _():
        for_pages(b + 1, (b + 1) % 2, True)

    slot = b % 2
    for_pages(b, slot, False)
    return slot


def _softmax_all(s_ref, p_ref):
    s3 = s_ref[...]
    m = jnp.max(jnp.max(s3, axis=0), axis=-1, keepdims=True)
    p3 = jnp.exp(s3 - m[None])
    p_ref[...] = p3.astype(BF16)
    return jnp.sum(jnp.sum(p3, axis=0), axis=-1, keepdims=True)


def _fox_sample_kernel(pt_ref, qbd_ref, lcol_ref, lrow_ref, knew_ref, vnew_ref, sup_ref, ck_hbm, cv_hbm, clf_hbm,
                       o_ref, kbuf, vbuf, lbuf, r_ref, s_ref, p_ref, sem, *, n_pages, t_new):
    slot = _page_pipeline(pt_ref, (ck_hbm, cv_hbm, clf_hbm), (kbuf, vbuf, lbuf), sem, n_pages)
    nh = FOX_HEADS
    rows = t_new * nh
    page = kbuf.shape[-1]
    qbd = qbd_ref[0]
    lc = lcol_ref[0]
    parts, run = [], None
    for t in range(t_new):
        run = lc[t * nh:(t + 1) * nh] if run is None else run + lc[t * nh:(t + 1) * nh]
        parts.append(run)
    qc = jnp.concatenate(parts, axis=0)

    lp_all = lbuf[slot].reshape(n_pages * nh, page)
    hi, mid, lo = _split3(lp_all)
    sup = sup_ref[...]
    r_in = _dot(hi, sup) + _dot(mid, sup) + _dot(lo, sup)
    tot = r_in[:, 0:1] + lp_all[:, 0:1]
    later, c = [], jnp.zeros((nh, 1), F32)
    for pg in reversed(range(n_pages)):
        later.append(c)
        c = c + tot[pg * nh:(pg + 1) * nh]
    r_ref[...] = r_in + jnp.concatenate(later[::-1], axis=0)

    def qk(pg, carry):
        r_pg = r_ref[pl.ds(pl.multiple_of(pg * nh, nh), nh), :]
        s_ref[pg] = _dot(qbd, kbuf[slot, pg].astype(BF16)) + jnp.concatenate([r_pg] * t_new, axis=0) + qc
        return carry

    lax.fori_loop(0, n_pages, qk, 0, unroll=PAGE_UNROLL)

    kn = _pad_rows(knew_ref[0], page).astype(BF16)
    vn = _pad_rows(vnew_ref[0], page).astype(BF16)
    lr = lrow_ref[0]
    lane8 = lax.broadcasted_iota(jnp.int32, lr.shape, 1)
    qct = jnp.zeros(lr.shape, F32)
    for t in range(t_new):
        qct = qct + jnp.where(lane8 >= t, lr[:, t:t + 1], 0.0)
    rr = lax.broadcasted_iota(jnp.int32, (rows, page), 0)
    cc = lax.broadcasted_iota(jnp.int32, (rows, page), 1)
    s_new = _dot_nt(qbd, kn) + qc - jnp.concatenate([qct] * t_new, axis=0)
    s_ref[n_pages] = jnp.where((cc * nh <= rr) & (cc < t_new), s_new, NEG)

    l = _softmax_all(s_ref, p_ref)

    def pv(pg, acc):
        return acc + _dot_nt(p_ref[pg], vbuf[slot, pg].astype(BF16))

    acc = lax.fori_loop(0, n_pages, pv, _dot(p_ref[n_pages], vn), unroll=PAGE_UNROLL)
    o = acc / l
    o_a, o_b = o[:, :LANES], o[:, LANES:]
    kv = (lax.broadcasted_iota(jnp.int32, (rows, LANES), 0) % nh) // (FOX_HEADS // FOX_KV_HEADS)
    sel = jnp.where(kv == 0, o_a, jnp.where(kv == 1, pltpu.roll(o_a, HEAD_DIM, 1),
                    jnp.where(kv == 2, o_b, pltpu.roll(o_b, HEAD_DIM, 1))))
    o_ref[0] = sel[:, :HEAD_DIM]


def _fox_sample(pt_flat, qbd, lcol, lrow, knew, vnew, ckt, cvt, clf, n_pages):
    db, rows, _ = qbd.shape
    t_new = rows // FOX_HEADS
    page = ckt.shape[-1]
    sup = jnp.asarray(np.tril(np.ones((page, page), np.float32), -1), dtype=BF16)
    per_b = lambda shape: pl.BlockSpec((1,) + shape, lambda b, pt: (b, 0, 0))
    hbm = pl.BlockSpec(memory_space=pl.ANY)
    in_specs = [per_b((rows, MXU_DIM)), per_b((rows, 1)), per_b((FOX_HEADS, LANES)),
                per_b((8, FOX_KV_WIDTH)), per_b((8, FOX_KV_WIDTH)),
                pl.BlockSpec((page, page), lambda b, pt: (0, 0)), hbm, hbm, hbm]
    return pl.pallas_call(
        functools.partial(_fox_sample_kernel, n_pages=n_pages, t_new=t_new),
        out_shape=jax.ShapeDtypeStruct((db, rows, HEAD_DIM), F32),
        grid_spec=pltpu.PrefetchScalarGridSpec(
            num_scalar_prefetch=1,
            grid=(db,),
            in_specs=in_specs,
            out_specs=pl.BlockSpec((1, rows, HEAD_DIM), lambda b, pt: (b, 0, 0)),
            scratch_shapes=[pltpu.VMEM((2, n_pages, FOX_KV_WIDTH, page), F32),
                            pltpu.VMEM((2, n_pages, FOX_KV_WIDTH, page), F32),
                            pltpu.VMEM((2, n_pages, FOX_HEADS, page), F32),
                            pltpu.VMEM((n_pages * FOX_HEADS, page), F32),
                            pltpu.VMEM((n_pages + 1, rows, page), F32),
                            pltpu.VMEM((n_pages + 1, rows, page), BF16),
                            pltpu.SemaphoreType.DMA((3, 2))]),
        compiler_params=pltpu.CompilerParams(dimension_semantics=("arbitrary",),
                                             vmem_limit_bytes=VMEM_LIMIT),
        name="fox_sample",
    )(pt_flat, qbd, lcol, lrow, knew, vnew, sup, ckt, cvt, clf)


def _diff_sample_kernel(pt_ref, qbd_ref, knew_ref, vnew_ref, dl_ref, ck_hbm, cv_hbm,
                        o_ref, kbuf, vbuf, s_ref, p_ref, sem, *, n_pages, t_new, lam_init):
    slot = _page_pipeline(pt_ref, (ck_hbm, cv_hbm), (kbuf, vbuf), sem, n_pages)
    half = t_new * DIFF_HEADS
    rows = 2 * half
    page = kbuf.shape[-1]
    qbd = qbd_ref[0]

    def qk(pg, carry):
        s_ref[pg] = _dot(qbd, kbuf[slot, pg].astype(BF16))
        return carry

    lax.fori_loop(0, n_pages, qk, 0, unroll=PAGE_UNROLL)

    kn = _pad_rows(knew_ref[0], page).astype(BF16)
    vn = _pad_rows(vnew_ref[0], page).astype(BF16)
    rr = lax.broadcasted_iota(jnp.int32, (rows, page), 0)
    cc = lax.broadcasted_iota(jnp.int32, (rows, page), 1)
    t_row = (rr % half) // DIFF_HEADS
    s_ref[n_pages] = jnp.where((cc <= t_row) & (cc < t_new), _dot_nt(qbd, kn), NEG)

    l = _softmax_all(s_ref, p_ref)

    def pv(pg, acc):
        a0, a1 = acc
        pp = p_ref[pg]
        v0 = vbuf[slot, pg, pl.ds(0, page, stride=DIFF_KV_HEADS), :].astype(BF16)
        v1 = vbuf[slot, pg, pl.ds(1, page, stride=DIFF_KV_HEADS), :].astype(BF16)
        return a0 + _dot(pp, v0), a1 + _dot(pp, v1)

    w = 2 * HEAD_DIM
    new = _dot(p_ref[n_pages], vn)
    a0, a1 = lax.fori_loop(0, n_pages, pv, (new[:, :w], new[:, w:]), unroll=PAGE_UNROLL)
    kv = (lax.broadcasted_iota(jnp.int32, (rows, w), 0) % DIFF_HEADS) // (DIFF_HEADS // DIFF_KV_HEADS)
    sel = jnp.where(kv == 0, a0, a1) / l
    o_ref[0] = sel[:half] - _diff_lambda(dl_ref, lam_init) * sel[half:]


def _diff_sample(pt_flat, qbd, knew, vnew, dlam, ckt, cv, n_pages, lam_init):
    db, rows, _ = qbd.shape
    t_new = rows // (2 * DIFF_HEADS)
    page = ckt.shape[-1]
    per_b = lambda shape: pl.BlockSpec((1,) + shape, lambda b, pt: (b, 0, 0))
    hbm = pl.BlockSpec(memory_space=pl.ANY)
    in_specs = [per_b((rows, MXU_DIM)), per_b((8, DIFF_KV_WIDTH)), per_b((8, DIFF_KV_WIDTH)),
                pl.BlockSpec((4, HEAD_DIM), lambda b, pt: (0, 0)), hbm, hbm]
    return pl.pallas_call(
        functools.partial(_diff_sample_kernel, n_pages=n_pages, t_new=t_new, lam_init=lam_init),
        out_shape=jax.ShapeDtypeStruct((db, rows // 2, 2 * HEAD_DIM), F32),
        grid_spec=pltpu.PrefetchScalarGridSpec(
            num_scalar_prefetch=1,
            grid=(db,),
            in_specs=in_specs,
            out_specs=pl.BlockSpec((1, rows // 2, 2 * HEAD_DIM), lambda b, pt: (b, 0, 0)),
            scratch_shapes=[pltpu.VMEM((2, n_pages, DIFF_KV_WIDTH, page), F32),
                            pltpu.VMEM((2, n_pages, DIFF_KV_HEADS * page, 2 * HEAD_DIM), F32),
                            pltpu.VMEM((n_pages + 1, rows, page), F32),
                            pltpu.VMEM((n_pages + 1, rows, page), BF16),
                            pltpu.SemaphoreType.DMA((2, 2))]),
        compiler_params=pltpu.CompilerParams(dimension_semantics=("arbitrary",),
                                             vmem_limit_bytes=VMEM_LIMIT),
        name="diff_sample",
    )(pt_flat, qbd, knew, vnew, dlam, ckt, cv)


def _heads_rmsnorm_t(xt, width, gain):
    parts = []
    for h in range(xt.shape[0] // width):
        xh = xt[h * width:(h + 1) * width]
        parts.append(xh * lax.rsqrt(jnp.mean(xh * xh, axis=0, keepdims=True) + EPS))
    return jnp.concatenate(parts, axis=0) * jnp.concatenate([gain] * (xt.shape[1] // LANES), axis=1)


def _ffn_kernel(*refs, tm, d_ff, post_scale, stateful):
    if stateful:
        (x_ref, fo_ref, do_ref, gm64_ref, gm128_ref, gfo_ref, gdo_ref, wo_ref, gffn_ref, wup_ref,
         cw_ref, cb_ref, wdn_ref, s0_ref, s1_ref, y_ref, h_ref, hs_ref) = refs
        fo = _group_rmsnorm(fo_ref[0], gm64_ref[...], gfo_ref[...])
        do = _group_rmsnorm(do_ref[0], gm128_ref[...], gdo_ref[...]) * post_scale
        merged = jnp.concatenate([fo, do], axis=1)
    else:
        (x_ref, head_ref, fo_ref, do_ref, gfo_ref, gdo_ref, wo_ref, gffn_ref, wup_ref,
         cw_ref, cb_ref, wdn_ref, y_ref, h_ref, hs_ref, carry_ref) = refs

        @pl.when(pl.program_id(1) == 0)
        def _():
            carry_ref[...] = jnp.zeros_like(carry_ref)

        fo = _heads_rmsnorm_t(fo_ref[0], HEAD_DIM, gfo_ref[...])
        do = _heads_rmsnorm_t(do_ref[0], 2 * HEAD_DIM, gdo_ref[...]) * post_scale
        merged = jnp.concatenate([fo, do], axis=0).T

    att = _dot(merged.astype(BF16), wo_ref[...])
    x = x_ref[0]
    if not stateful:
        x = jnp.where(pl.program_id(1) == 0, head_ref[...], x)
    x1 = x + att
    hn = x1 * lax.rsqrt(jnp.mean(x1 * x1, axis=-1, keepdims=True) + EPS) * gffn_ref[...]
    hnb = hn.astype(BF16)
    ch = FF_CHUNK
    n_chunks = d_ff // ch
    rid = lax.broadcasted_iota(jnp.int32, (8, ch), 0)

    def produce(c):
        for half in range(2):
            col = half * d_ff + c * ch
            hs_ref[c % 2, half] = _dot(hnb, wup_ref[:, col:col + ch])

    def consume(c, acc):
        conv = []
        for half in range(2):
            col = half * d_ff + c * ch
            h = hs_ref[c % 2, half]
            if stateful:
                nseq = s0_ref.shape[0]
                hm1 = jnp.concatenate([s1_ref[:, col:col + ch], h[:tm - nseq]], axis=0)
                hm2 = jnp.concatenate([s0_ref[:, col:col + ch], s1_ref[:, col:col + ch], h[:tm - 2 * nseq]], axis=0)
                h_ref[:, col:col + ch] = h[tm - 2 * nseq:]
            else:
                prev = carry_ref[:, col:col + ch]
                r1, r2 = pltpu.roll(h, 1, 0), pltpu.roll(h, 2, 0)
                top1 = jnp.where(rid < 1, pltpu.roll(prev, 1, 0), r1[0:8])
                top2 = jnp.where(rid < 2, pltpu.roll(prev, 2, 0), r2[0:8])
                hm1 = jnp.concatenate([top1, r1[8:]], axis=0)
                hm2 = jnp.concatenate([top2, r2[8:]], axis=0)
                carry_ref[:, col:col + ch] = h[tm - 8:]
                h_ref[0, :, col:col + ch] = h[tm - 8:]
            conv.append(cb_ref[:, col:col + ch] + cw_ref[0:1, col:col + ch] * hm2
                        + cw_ref[1:2, col:col + ch] * hm1 + cw_ref[2:3, col:col + ch] * h)
        g, u = conv
        act = g * (1.0 / (1.0 + jnp.exp(-g))) * u
        return acc + _dot(act.astype(BF16), wdn_ref[c * ch:(c + 1) * ch, :])

    acc = x1
    produce(0)
    for c in range(n_chunks):
        if c + 1 < n_chunks:
            produce(c + 1)
        acc = consume(c, acc)
    y_ref[0] = acc


def _merge_ffn(x, fo, do, p, tm, state=None, head=None):
    nb, lt, d = x.shape
    d_ff = p["w_down"].shape[0]
    stateful = state is not None
    single = pl.Buffered(1)
    const = lambda shape: pl.BlockSpec(shape, lambda b, i: (0,) * len(shape), pipeline_mode=single)
    weights = [const((FOX_WIDTH + DIFF_WIDTH, d)), const((1, d)), const((d, 2 * d_ff)), const((CONV_W, 2 * d_ff)),
               const((1, 2 * d_ff)), const((d_ff, d))]
    wargs = [p["w_o"], p["g_ffn"], p["w_up"], p["conv_w"], p["conv_b"], p["w_down"]]
    if stateful:
        nseq = state[0].shape[0]
        row = lambda w: pl.BlockSpec((1, tm, w), lambda b, i: (b, i, 0), pipeline_mode=single)
        in_specs = ([row(d), row(FOX_WIDTH), row(DIFF_WIDTH), const((MXU_DIM, MXU_DIM)), const((MXU_DIM, MXU_DIM)),
                     const((1, FOX_WIDTH)), const((1, DIFF_WIDTH))] + weights
                    + [const((nseq, 2 * d_ff)), const((nseq, 2 * d_ff))])
        args = [x, fo, do, p["gm64"], p["gm128"], p["g_fo"], p["g_do"]] + wargs + list(state)
        h_shape = jax.ShapeDtypeStruct((2 * nseq, 2 * d_ff), F32)
        h_spec = pl.BlockSpec((2 * nseq, 2 * d_ff), lambda b, i: (0, 0))
        scratch = [pltpu.VMEM((2, 2, tm, FF_CHUNK), F32)]
    else:
        real = pl.BlockSpec((1, tm, d), lambda b, i: (b, jnp.maximum(i - 1, 0), 0))
        in_specs = ([real, const((tm, d)),
                     pl.BlockSpec((1, FOX_WIDTH, tm), lambda b, i: (b, 0, i)),
                     pl.BlockSpec((1, DIFF_WIDTH, tm), lambda b, i: (b, 0, i)),
                     const((FOX_WIDTH, LANES)), const((DIFF_WIDTH, LANES))] + weights)
        args = [x, head, fo, do, p["g_fo_t"], p["g_do_t"]] + wargs
        h_shape = jax.ShapeDtypeStruct((nb, 8, 2 * d_ff), F32)
        h_spec = pl.BlockSpec((1, 8, 2 * d_ff), lambda b, i: (b, 0, 0))
        scratch = [pltpu.VMEM((2, 2, tm, FF_CHUNK), F32), pltpu.VMEM((8, 2 * d_ff), F32)]
    y_spec = pl.BlockSpec((1, tm, d), lambda b, i: (b, i, 0)) if stateful else real
    n_tiles = lt // tm + (0 if stateful else 1)
    kern = functools.partial(_ffn_kernel, tm=tm, d_ff=d_ff, post_scale=p["post_scale"], stateful=stateful)
    return pl.pallas_call(
        kern,
        out_shape=[jax.ShapeDtypeStruct((nb, lt, d), F32), h_shape],
        grid=(nb, n_tiles),
        in_specs=in_specs,
        out_specs=[y_spec, h_spec],
        scratch_shapes=scratch,
        compiler_params=pltpu.CompilerParams(dimension_semantics=("parallel", "arbitrary"),
                                             vmem_limit_bytes=VMEM_LIMIT),
        name="merge_ffn_sample" if stateful else "merge_ffn_prompt",
    )(*args)


def kernel(x_prompt, x_sample, cache_fox_k, cache_fox_v, cache_fox_logf, cache_diff_k, cache_diff_v, state_ffn_conv, page_table, meta_tokens, attn_norm_g, w_in, b_f, fox_qn_g, fox_kn_g, fox_on_g, diff_qn_g, diff_kn_g, diff_lambda, diff_subln_g, w_o, ffn_norm_g, w_up, conv_w, conv_b, w_down):
    nb, seq, d = x_prompt.shape
    db, t_new, _ = x_sample.shape
    depth, n_phys, page = cache_fox_k.shape[:3]
    n_pages = page_table.shape[1]
    assert depth == 1, "single layer only"
    layer = 0
    lam_init = 0.8 - 0.6 * math.exp(-0.3 * layer)
    past_len = n_pages * page

    wi = w_in[layer]
    o = np.cumsum([0, FOX_WIDTH, FOX_KV_WIDTH, FOX_KV_WIDTH, FOX_HEADS, DIFF_WIDTH, DIFF_KV_WIDTH, DIFF_KV_WIDTH])
    seg = lambda k: wi[:, o[k]:o[k + 1]]
    w_all = jnp.concatenate([seg(0), seg(1), seg(2), seg(4), seg(5), seg(6),
                             jnp.pad(seg(3), ((0, 0), (0, LANES - FOX_HEADS)))], axis=1).astype(BF16)
    g_fo = fox_on_g[layer].reshape(1, FOX_WIDTH)
    g_do = jnp.tile(diff_subln_g[layer], DIFF_HEADS)[None, :]
    lane_rep = lambda g: jnp.broadcast_to(g.reshape(-1, 1), (g.shape[-1], LANES))
    p = {
        "g_attn": attn_norm_g[layer][None, :],
        "w_in": w_all,
        "gm64": _block_diag_mean(HEAD_DIM),
        "gm128": _block_diag_mean(2 * HEAD_DIM),
        "g_heads": jnp.concatenate([jnp.tile(fox_qn_g[layer], FOX_HEADS), jnp.tile(fox_kn_g[layer], FOX_KV_HEADS),
                                    jnp.tile(diff_qn_g[layer], 2 * DIFF_HEADS),
                                    jnp.tile(diff_kn_g[layer], 2 * DIFF_KV_HEADS)])[None, :],
        "b_f": jnp.pad(b_f[layer], (0, LANES - FOX_HEADS))[None, :],
        "g_fo": g_fo, "g_do": g_do, "g_fo_t": lane_rep(g_fo), "g_do_t": lane_rep(g_do),
        "post_scale": 1.0 - lam_init,
        "w_o": w_o[layer].astype(BF16),
        "g_ffn": ffn_norm_g[layer][None, :],
        "w_up": w_up[layer].astype(BF16),
        "conv_w": conv_w[layer],
        "conv_b": conv_b[layer][None, :],
        "w_down": w_down[layer].astype(BF16),
    }
    dlam = diff_lambda[layer]

    tm = ROW_TILE
    pad = (-N_META) % tm
    lp = pad + N_META + seq
    assert pad + N_META == tm and seq % tm == 0
    head = jnp.concatenate([jnp.zeros((pad, d), x_prompt.dtype), meta_tokens.astype(x_prompt.dtype)], axis=0)
    pos_p = jnp.arange(lp) - pad
    fk, fv, lf, dk, dv, qat, ka, vt, dqt, dka, dvt = _project(x_prompt, pos_p, tm, p, head=head)
    fo_t = _fox_prompt(qat, ka, vt, tm, pad)
    do_t = _diff_prompt(dqt, dka, dvt, dlam, tm, pad, lam_init)
    y_prompt, hlast = _merge_ffn(x_prompt, fo_t, do_t, p, tm, head=head)
    ltot = N_META + seq
    pk = fk[:, pad:].reshape(1, nb, ltot, FOX_KV_HEADS, HEAD_DIM)
    pv = fv[:, pad:].reshape(1, nb, ltot, FOX_KV_HEADS, HEAD_DIM)
    pf = lf[:, pad:][None]
    pdk = dk[:, pad:].reshape(1, nb, ltot, DIFF_KV_HEADS, 2, HEAD_DIM)
    pdv = dv[:, pad:].reshape(1, nb, ltot, DIFF_KV_HEADS, 2 * HEAD_DIM)
    pc = hlast[:, 8 - (CONV_W - 1):][None]

    rows = db * t_new
    pos_s = past_len + (jnp.arange(rows) % t_new)
    qf, fk, fv, lf, dqb, dk, dv = _project(x_sample.reshape(1, rows, d), pos_s, min(ROW_TILE, rows), p)
    g_f = FOX_HEADS // FOX_KV_HEADS
    g_d = DIFF_HEADS // DIFF_KV_HEADS
    eye_f = jnp.eye(FOX_KV_HEADS, dtype=BF16)
    qbd_f = jnp.einsum('btkgd,kj->btkgjd', qf.reshape(db, t_new, FOX_KV_HEADS, g_f, HEAD_DIM), eye_f)
    qbd_f = qbd_f.reshape(db, t_new * FOX_HEADS, FOX_KV_WIDTH)
    eye_d, eye_2 = jnp.eye(DIFF_KV_HEADS, dtype=BF16), jnp.eye(2, dtype=BF16)
    qbd_d = jnp.einsum('btkgwd,kj,wv->bwtkgjvd', dqb.reshape(db, t_new, DIFF_KV_HEADS, g_d, 2, HEAD_DIM), eye_d, eye_2)
    qbd_d = qbd_d.reshape(db, 2 * t_new * DIFF_HEADS, DIFF_KV_WIDTH)
    lf_s = lf.reshape(db, t_new, FOX_HEADS)
    lcol = lf_s.reshape(db, t_new * FOX_HEADS, 1)
    lrow = jnp.pad(jnp.swapaxes(lf_s, 1, 2), ((0, 0), (0, 0), (0, LANES - t_new)))
    new_rows = lambda a: jnp.pad(a.reshape(db, t_new, a.shape[-1]), ((0, 0), (0, 8 - t_new), (0, 0)))
    pt_flat = page_table.reshape(-1).astype(jnp.int32)
    ckt = jnp.transpose(cache_fox_k[layer], (0, 2, 3, 1)).reshape(n_phys, FOX_KV_WIDTH, page)
    cvt = jnp.transpose(cache_fox_v[layer], (0, 2, 3, 1)).reshape(n_phys, FOX_KV_WIDTH, page)
    clf = jnp.swapaxes(cache_fox_logf[layer], 1, 2)
    fo_s = _fox_sample(pt_flat, qbd_f, lcol, lrow, new_rows(fk), new_rows(fv), ckt, cvt, clf, n_pages)
    cdkt = jnp.transpose(cache_diff_k[layer], (0, 2, 3, 4, 1)).reshape(n_phys, DIFF_KV_WIDTH, page)
    cdv = cache_diff_v[layer].reshape(n_phys, page * DIFF_KV_HEADS, 2 * HEAD_DIM)
    do_s = _diff_sample(pt_flat, qbd_d, new_rows(dk), new_rows(dv), dlam, cdkt, cdv, n_pages, lam_init)

    tmajor = lambda a, w: jnp.swapaxes(a.reshape(db, t_new, w), 0, 1).reshape(1, rows, w)
    st = state_ffn_conv[layer]
    ys, hs = _merge_ffn(tmajor(x_sample, d), tmajor(fo_s, FOX_WIDTH), tmajor(do_s, DIFF_WIDTH), p, rows,
                        state=(st[:, 0], st[:, 1]))
    y_sample = jnp.swapaxes(ys.reshape(t_new, db, d), 0, 1)
    sc = jnp.swapaxes(hs.reshape(CONV_W - 1, db, hs.shape[-1]), 0, 1)[None]
    sk = fk.reshape(1, db, t_new, FOX_KV_HEADS, HEAD_DIM)
    sv = fv.reshape(1, db, t_new, FOX_KV_HEADS, HEAD_DIM)
    sf = lf_s[None]
    sdk = dk.reshape(1, db, t_new, DIFF_KV_HEADS, 2, HEAD_DIM)
    sdv = dv.reshape(1, db, t_new, DIFF_KV_HEADS, 2 * HEAD_DIM)
    return (y_prompt, y_sample, pk, pv, pf, pdk, pdv, pc, sk, sv, sf, sdk, sdv, sc)
```

```python
import functools
import math

import numpy as np
import jax
import jax.numpy as jnp
from jax import lax
from jax.experimental import pallas as pl
from jax.experimental.pallas import tpu as pltpu

F32 = jnp.float32
BF16 = jnp.bfloat16

HEAD_DIM = 64
FOX_HEADS = 8
FOX_KV_HEADS = 4
DIFF_HEADS = 4
DIFF_KV_HEADS = 2
FOX_WIDTH = FOX_HEADS * HEAD_DIM
FOX_KV_WIDTH = FOX_KV_HEADS * HEAD_DIM
DIFF_WIDTH = DIFF_HEADS * 2 * HEAD_DIM
DIFF_KV_WIDTH = DIFF_KV_HEADS * 2 * HEAD_DIM
ROT_DIM = HEAD_DIM // 4
ROPE_THETA = 500000.0
CONV_W = 3
N_META = 16
EPS = 1e-6
NEG = -1e30
SCALE = HEAD_DIM ** -0.5
LOG2E = 1.4426950408889634

LANES = 128
MXU_DIM = 256
ROW_TILE = 256
FF_CHUNK = 256
PAGE_UNROLL = 16
VMEM_LIMIT = 56 * 1024 * 1024
VT_PAD = 16
FOX_VT_ROWS = FOX_KV_HEADS * (HEAD_DIM + VT_PAD)
DIFF_VT_ROWS = DIFF_KV_HEADS * (2 * HEAD_DIM + VT_PAD)

_FQ, _FK, _FV, _DQ, _DK, _DV, _FF, _W_TOTAL = 0, 512, 768, 1024, 1536, 1792, 2048, 2176
_AUG_C, _AUG_G0 = HEAD_DIM, HEAD_DIM + 3


def _split3(x):
    hi = x.astype(BF16)
    r = x - hi.astype(F32)
    mid = r.astype(BF16)
    lo = (r - mid.astype(F32)).astype(BF16)
    return hi, mid, lo


def _dot(a, b):
    return jnp.dot(a, b, preferred_element_type=F32)


def _dot_nt(a, b):
    return lax.dot_general(a, b, (((1,), (1,)), ((), ())), preferred_element_type=F32)


def _group_mean_sq(x, gmat):
    outs = []
    for c in range(x.shape[1] // MXU_DIM):
        s = x[:, c * MXU_DIM:(c + 1) * MXU_DIM]
        outs.append(_dot((s * s).astype(BF16), gmat))
    return outs[0] if len(outs) == 1 else jnp.concatenate(outs, axis=1)


def _group_rmsnorm(x, gmat, gain):
    return x * lax.rsqrt(_group_mean_sq(x, gmat) + EPS) * gain


def _proj_kernel(*refs, prompt):
    (x_ref, gattn_ref, w_ref, gm_ref, gh_ref, bf_ref, cos_ref, sa_ref, sb_ref) = refs[:9]
    if prompt:
        (tri_ref, pq_ref, pk_ref, oq_ref, ok_ref, head_ref,
         fk_ref, fv_ref, lf_ref, dk_ref, dv_ref,
         qat_ref, ka_ref, vt_ref, dqt_ref, dka_ref, dvt_ref, carry_ref) = refs[9:]
    else:
        (qf_ref, fk_ref, fv_ref, lf_ref, dqb_ref, dk_ref, dv_ref) = refs[9:]
    x = x_ref[0]
    if prompt:
        x = jnp.where(pl.program_id(1) == 0, head_ref[...], x)
    xn = x * lax.rsqrt(jnp.mean(x * x, axis=-1, keepdims=True) + EPS) * gattn_ref[...]
    y = _dot(xn.astype(BF16), w_ref[...])
    gm = gm_ref[...]
    cos_t, sin_a, sin_b = cos_ref[...], sa_ref[...], sb_ref[...]

    def rope(seg):
        outs = []
        for c in range(seg.shape[1] // LANES):
            s = seg[:, c * LANES:(c + 1) * LANES]
            outs.append(s * cos_t + pltpu.roll(s, LANES - ROT_DIM // 2, 1) * sin_a
                        + pltpu.roll(s, ROT_DIM // 2, 1) * sin_b)
        return jnp.concatenate(outs, axis=1)

    fq = _group_rmsnorm(y[:, _FQ:_FK], gm, gh_ref[:, 0:512])
    fk = _group_rmsnorm(y[:, _FK:_FV], gm, gh_ref[:, 512:768])
    fv = y[:, _FV:_DQ]
    dq = rope(_group_rmsnorm(y[:, _DQ:_DK], gm, gh_ref[:, 768:1280]))
    dk = rope(_group_rmsnorm(y[:, _DK:_DV], gm, gh_ref[:, 1280:1536]))
    dv = y[:, _DV:_FF]
    z = y[:, _FF:_W_TOTAL] + bf_ref[...]
    lf = jnp.minimum(z, 0.0) - jnp.log1p(jnp.exp(-jnp.abs(z)))
    fk_ref[0] = fk
    fv_ref[0] = fv
    dk_ref[0] = dk
    dv_ref[0] = dv
    lf_ref[0] = lf[:, :FOX_HEADS]
    if not prompt:
        qf_ref[0] = (fq * SCALE).astype(BF16)
        dqb_ref[0] = (dq * SCALE).astype(BF16)
        return

    lane = lax.broadcasted_iota(jnp.int32, lf.shape, 1)
    lf = jnp.where(lane < FOX_HEADS, lf, 0.0)
    hi, mid, lo = _split3(lf)
    tri = tri_ref[...]
    cs = _dot(tri, hi) + _dot(tri, mid) + _dot(tri, lo)

    @pl.when(pl.program_id(1) == 0)
    def _():
        carry_ref[...] = jnp.zeros_like(carry_ref)

    c = cs + carry_ref[0:1, :]
    tm = c.shape[0]
    carry_ref[0:1, :] = c[tm - 1:tm, :]

    x3 = jnp.concatenate(_split3(c * LOG2E), axis=1)
    eq = _dot(x3, pq_ref[...]) + oq_ref[...]
    ek = _dot(x3, pk_ref[...]) + ok_ref[...]
    low = lane < HEAD_DIM
    half = lambda chunk, odd: pltpu.roll(chunk, HEAD_DIM, 1) if odd else chunk
    fqs = fq * (SCALE * LOG2E)
    for h in range(FOX_HEADS):
        chunk = fqs[:, (h // 2) * LANES:(h // 2 + 1) * LANES]
        qa = jnp.where(low, half(chunk, h % 2), eq[:, h * LANES:(h + 1) * LANES])
        qat_ref[0, h * LANES:(h + 1) * LANES, :] = qa.T.astype(BF16)
    for h in range(FOX_KV_HEADS):
        chunk = fk[:, (h // 2) * LANES:(h // 2 + 1) * LANES]
        ka = jnp.where(low, half(chunk, h % 2), ek[:, h * LANES:(h + 1) * LANES])
        ka_ref[0, :, h * LANES:(h + 1) * LANES] = ka.astype(BF16)
    ones_rows = jnp.where(lax.broadcasted_iota(jnp.int32, (VT_PAD, tm), 0) == 0, 1.0, 0.0).astype(BF16)
    fw = HEAD_DIM + VT_PAD
    for c2 in range(FOX_KV_WIDTH // LANES):
        t = fv[:, c2 * LANES:(c2 + 1) * LANES].T.astype(BF16)
        for k in range(LANES // HEAD_DIM):
            h = c2 * (LANES // HEAD_DIM) + k
            vt_ref[0, h * fw:h * fw + HEAD_DIM, :] = t[k * HEAD_DIM:(k + 1) * HEAD_DIM]
            vt_ref[0, h * fw + HEAD_DIM:(h + 1) * fw, :] = ones_rows
    dqs = dq * (SCALE * LOG2E)
    for c2 in range(DIFF_WIDTH // LANES):
        dqt_ref[0, c2 * LANES:(c2 + 1) * LANES, :] = dqs[:, c2 * LANES:(c2 + 1) * LANES].T.astype(BF16)
    for h in range(2 * DIFF_KV_HEADS):
        chunk = dk[:, (h // 2) * LANES:(h // 2 + 1) * LANES]
        dka_ref[0, :, h * LANES:(h + 1) * LANES] = jnp.where(low, half(chunk, h % 2), 0.0).astype(BF16)
    dw = 2 * HEAD_DIM + VT_PAD
    for h in range(DIFF_KV_HEADS):
        dvt_ref[0, h * dw:h * dw + 2 * HEAD_DIM, :] = dv[:, h * LANES:(h + 1) * LANES].T.astype(BF16)
        dvt_ref[0, h * dw + 2 * HEAD_DIM:(h + 1) * dw, :] = ones_rows


def _rope_tables(pos):
    inv_freq = ROPE_THETA ** (-jnp.arange(0, ROT_DIM, 2, dtype=F32) / ROT_DIM)
    ang = pos.astype(F32)[:, None] * inv_freq[None, :]
    cos, sin = jnp.cos(ang), jnp.sin(ang)
    half = ROT_DIM // 2
    ones = jnp.ones((pos.shape[0], HEAD_DIM - ROT_DIM), F32)
    zeros_h = jnp.zeros((pos.shape[0], half), F32)
    zeros_r = jnp.zeros((pos.shape[0], HEAD_DIM - ROT_DIM), F32)
    cos_h = jnp.concatenate([cos, cos, ones], axis=1)
    sa_h = jnp.concatenate([-sin, zeros_h, zeros_r], axis=1)
    sb_h = jnp.concatenate([zeros_h, sin, zeros_r], axis=1)
    rep = LANES // HEAD_DIM
    return jnp.tile(cos_h, (1, rep)), jnp.tile(sa_h, (1, rep)), jnp.tile(sb_h, (1, rep))


def _block_diag_mean(group):
    idx = np.arange(MXU_DIM) // group
    return jnp.asarray((idx[:, None] == idx[None, :]).astype(np.float32) / group, dtype=BF16)


def _bias_placement():
    g = FOX_HEADS // FOX_KV_HEADS
    pq = np.zeros((3 * LANES, FOX_HEADS * LANES), np.float32)
    pk = np.zeros((3 * LANES, FOX_KV_HEADS * LANES), np.float32)
    oq = np.zeros((1, FOX_HEADS * LANES), np.float32)
    ok = np.zeros((1, FOX_KV_HEADS * LANES), np.float32)
    for h in range(FOX_HEADS):
        kvh, gi = divmod(h, g)
        for piece in range(3):
            pq[piece * LANES + h, h * LANES + _AUG_C + piece] = 1.0
            pk[piece * LANES + h, kvh * LANES + _AUG_G0 + 3 * gi + piece] = -1.0
            oq[0, h * LANES + _AUG_G0 + 3 * gi + piece] = 1.0
            ok[0, kvh * LANES + _AUG_C + piece] = 1.0
    return jnp.asarray(pq, BF16), jnp.asarray(pk, BF16), jnp.asarray(oq), jnp.asarray(ok)


def _project(x, pos, tm, p, head=None):
    prompt = head is not None
    nb, lt, d = x.shape
    lt += tm if prompt else 0
    nt = lt // tm
    cos_t, sin_a, sin_b = _rope_tables(pos)
    row = lambda w, dt=None: pl.BlockSpec((1, tm, w), lambda b, i: (b, i, 0))
    col = lambda w: pl.BlockSpec((1, w, tm), lambda b, i: (b, 0, i))
    const = lambda shape: pl.BlockSpec(shape, lambda b, i: (0,) * len(shape))
    tab = pl.BlockSpec((tm, LANES), lambda b, i: (i, 0))
    sds = jax.ShapeDtypeStruct
    x_spec = pl.BlockSpec((1, tm, d), lambda b, i: (b, jnp.maximum(i - 1, 0), 0)) if prompt else row(d)
    in_specs = [x_spec, const((1, d)), const((d, _W_TOTAL)), const((MXU_DIM, MXU_DIM)),
                const((1, 1536)), const((1, LANES)), tab, tab, tab]
    args = [x, p["g_attn"], p["w_in"], p["gm64"], p["g_heads"], p["b_f"], cos_t, sin_a, sin_b]
    f32_rows = [sds((nb, lt, FOX_KV_WIDTH), F32), sds((nb, lt, FOX_KV_WIDTH), F32), sds((nb, lt, FOX_HEADS), F32),
                sds((nb, lt, DIFF_KV_WIDTH), F32), sds((nb, lt, DIFF_KV_WIDTH), F32)]
    f32_specs = [row(FOX_KV_WIDTH), row(FOX_KV_WIDTH), row(FOX_HEADS), row(DIFF_KV_WIDTH), row(DIFF_KV_WIDTH)]
    if prompt:
        tri = jnp.asarray(np.tril(np.ones((tm, tm), np.float32)), dtype=BF16)
        pq, pk, oq, ok = _bias_placement()
        in_specs += [const((tm, tm)), const(pq.shape), const(pk.shape), const(oq.shape), const(ok.shape),
                     const((tm, d))]
        args += [tri, pq, pk, oq, ok, head]
        out_shape = f32_rows + [sds((nb, FOX_HEADS * LANES, lt), BF16), sds((nb, lt, FOX_KV_HEADS * LANES), BF16),
                                sds((nb, FOX_VT_ROWS, lt), BF16), sds((nb, DIFF_WIDTH, lt), BF16),
                                sds((nb, lt, 2 * DIFF_KV_HEADS * LANES), BF16), sds((nb, DIFF_VT_ROWS, lt), BF16)]
        out_specs = f32_specs + [col(FOX_HEADS * LANES), row(FOX_KV_HEADS * LANES), col(FOX_VT_ROWS),
                                 col(DIFF_WIDTH), row(2 * DIFF_KV_HEADS * LANES), col(DIFF_VT_ROWS)]
        scratch = [pltpu.VMEM((8, LANES), F32)]
    else:
        out_shape = ([sds((nb, lt, FOX_WIDTH), BF16)] + f32_rows[:3] + [sds((nb, lt, DIFF_WIDTH), BF16)] + f32_rows[3:])
        out_specs = [row(FOX_WIDTH)] + f32_specs[:3] + [row(DIFF_WIDTH)] + f32_specs[3:]
        scratch = []
    return pl.pallas_call(
        functools.partial(_proj_kernel, prompt=prompt),
        out_shape=out_shape,
        grid=(nb, nt),
        in_specs=in_specs,
        out_specs=out_specs,
        scratch_shapes=scratch,
        compiler_params=pltpu.CompilerParams(dimension_semantics=("parallel", "arbitrary"),
                                             vmem_limit_bytes=VMEM_LIMIT),
        name="proj_prompt" if prompt else "proj_sample",
    )(*args)


def _reset(m_ref, acc_ref):
    m_ref[...] = jnp.full(m_ref.shape, NEG, F32)
    acc_ref[...] = jnp.zeros(acc_ref.shape, F32)


def _flash_block_t(st, vt, m_ref, acc_ref):
    m_prev = m_ref[...]
    m_new = jnp.maximum(m_prev, jnp.max(st, axis=0, keepdims=True))
    alpha = jnp.exp2(m_prev - m_new)
    p = jnp.exp2(st - m_new)
    acc_ref[...] = alpha * acc_ref[...] + _dot(vt, p.astype(BF16))
    m_ref[...] = m_new


def _normalised(acc, dims):
    return acc[:dims] / acc[dims:dims + 1]


def _causal_sweep(i, produce, consume):
    produce(0, 0)

    @pl.when(i == 0)
    def _():
        consume(0, 0, True)

    @pl.when(i > 0)
    def _():
        produce(1, 1)
        consume(0, 0, True)

        def body(jj, carry):
            j = 2 * jj + 1
            produce(j + 1, 0)
            consume(j, 1, False)

            @pl.when(j + 1 < i)
            def _():
                produce(j + 2, 1)
                consume(j + 1, 0, False)
            return carry

        lax.fori_loop(0, i // 2, body, 0)

        @pl.when(i % 2 == 1)
        def _():
            consume(i, 1, True)

        @pl.when(i % 2 == 0)
        def _():
            consume(i, 0, True)


def _key_mask(j, i, tq, nq, pad):
    krow = j * tq + lax.broadcasted_iota(jnp.int32, (tq, 1), 0)
    lane = lax.broadcasted_iota(jnp.int32, (1, nq), 1)
    qpos = i * tq + jnp.where(lane >= tq, lane - tq, lane)
    return (krow <= qpos) & (krow >= pad)


def _fox_prompt_kernel(qt_ref, ka_ref, vt_ref, o_ref, m_ref, acc_ref, s_ref, *, tq, pad):
    i = pl.program_id(1)
    hd = HEAD_DIM
    vw = HEAD_DIM + VT_PAD
    _reset(m_ref, acc_ref)

    def produce(j, slot):
        ks = pl.multiple_of(j * tq, tq)
        for kvh in range(FOX_KV_HEADS):
            h0, h1 = 2 * kvh, 2 * kvh + 1
            qt = jnp.concatenate([qt_ref[0, h0 * LANES:(h0 + 1) * LANES, :],
                                  qt_ref[0, h1 * LANES:(h1 + 1) * LANES, :]], axis=1)
            s_ref[slot, kvh] = _dot(ka_ref[0, pl.ds(ks, tq), kvh * LANES:(kvh + 1) * LANES], qt)

    def consume(j, slot, masked):
        ks = pl.multiple_of(j * tq, tq)
        mask = _key_mask(j, i, tq, 2 * tq, pad) if masked else None
        for kvh in range(FOX_KV_HEADS):
            st = s_ref[slot, kvh]
            if masked:
                st = jnp.where(mask, st, NEG)
            _flash_block_t(st, vt_ref[0, kvh * vw:(kvh + 1) * vw, pl.ds(ks, tq)], m_ref.at[kvh], acc_ref.at[kvh])

    _causal_sweep(i, produce, consume)
    for kvh in range(FOX_KV_HEADS):
        h0, h1 = 2 * kvh, 2 * kvh + 1
        o = _normalised(acc_ref[kvh], hd)
        o_ref[0, h0 * hd:(h0 + 1) * hd, :] = o[:, :tq]
        o_ref[0, h1 * hd:(h1 + 1) * hd, :] = o[:, tq:]


def _fox_prompt(qat, ka, vt, tq, pad):
    nb, _, lp = qat.shape
    return pl.pallas_call(
        functools.partial(_fox_prompt_kernel, tq=tq, pad=pad),
        out_shape=jax.ShapeDtypeStruct((nb, FOX_WIDTH, lp), F32),
        grid=(nb, lp // tq),
        in_specs=[pl.BlockSpec((1, FOX_HEADS * LANES, tq), lambda b, i: (b, 0, i)),
                  pl.BlockSpec((1, lp, FOX_KV_HEADS * LANES), lambda b, i: (b, 0, 0)),
                  pl.BlockSpec((1, FOX_VT_ROWS, lp), lambda b, i: (b, 0, 0))],
        out_specs=pl.BlockSpec((1, FOX_WIDTH, tq), lambda b, i: (b, 0, i)),
        scratch_shapes=[pltpu.VMEM((FOX_KV_HEADS, 1, 2 * tq), F32),
                        pltpu.VMEM((FOX_KV_HEADS, HEAD_DIM + VT_PAD, 2 * tq), F32),
                        pltpu.VMEM((2, FOX_KV_HEADS, tq, 2 * tq), F32)],
        compiler_params=pltpu.CompilerParams(dimension_semantics=("parallel", "arbitrary"),
                                             vmem_limit_bytes=VMEM_LIMIT),
        name="fox_prompt",
    )(qat, ka, vt)


def _diff_lambda(dl_ref, lam_init):
    dl = dl_ref[...]
    a = jnp.sum(dl[0:1] * dl[1:2], axis=-1, keepdims=True)
    b = jnp.sum(dl[2:3] * dl[3:4], axis=-1, keepdims=True)
    return jnp.exp(a) - jnp.exp(b) + lam_init


def _diff_prompt_kernel(qt_ref, ka_ref, vt_ref, dl_ref, o_ref, m_ref, acc_ref, s_ref, *, tq, pad, lam_init):
    i = pl.program_id(1)
    hd = HEAD_DIM
    vw = 2 * HEAD_DIM + VT_PAD
    _reset(m_ref, acc_ref)
    streams = [(kvh, w) for kvh in range(DIFF_KV_HEADS) for w in range(2)]

    def produce(j, slot):
        ks = pl.multiple_of(j * tq, tq)
        for n, (kvh, w) in enumerate(streams):
            r0 = (kvh * 4 + w) * hd
            qt = jnp.concatenate([qt_ref[0, r0:r0 + hd, :], qt_ref[0, r0 + 2 * hd:r0 + 3 * hd, :]], axis=1)
            s_ref[slot, n] = _dot(ka_ref[0, pl.ds(ks, tq), n * LANES:n * LANES + hd], qt)

    def consume(j, slot, masked):
        ks = pl.multiple_of(j * tq, tq)
        mask = _key_mask(j, i, tq, 2 * tq, pad) if masked else None
        for n, (kvh, w) in enumerate(streams):
            st = s_ref[slot, n]
            if masked:
                st = jnp.where(mask, st, NEG)
            _flash_block_t(st, vt_ref[0, kvh * vw:(kvh + 1) * vw, pl.ds(ks, tq)], m_ref.at[n], acc_ref.at[n])

    _causal_sweep(i, produce, consume)
    lam = _diff_lambda(dl_ref, lam_init)
    w = 2 * hd
    for kvh in range(DIFF_KV_HEADS):
        n1, n2 = 2 * kvh, 2 * kvh + 1
        o = _normalised(acc_ref[n1], w) - lam * _normalised(acc_ref[n2], w)
        o_ref[0, (2 * kvh) * w:(2 * kvh + 1) * w, :] = o[:, :tq]
        o_ref[0, (2 * kvh + 1) * w:(2 * kvh + 2) * w, :] = o[:, tq:]


def _diff_prompt(dqt, dka, dvt, dlam, tq, pad, lam_init):
    nb, _, lp = dqt.shape
    ns = 2 * DIFF_KV_HEADS
    return pl.pallas_call(
        functools.partial(_diff_prompt_kernel, tq=tq, pad=pad, lam_init=lam_init),
        out_shape=jax.ShapeDtypeStruct((nb, DIFF_WIDTH, lp), F32),
        grid=(nb, lp // tq),
        in_specs=[pl.BlockSpec((1, DIFF_WIDTH, tq), lambda b, i: (b, 0, i)),
                  pl.BlockSpec((1, lp, 2 * DIFF_KV_HEADS * LANES), lambda b, i: (b, 0, 0)),
                  pl.BlockSpec((1, DIFF_VT_ROWS, lp), lambda b, i: (b, 0, 0)),
                  pl.BlockSpec((4, HEAD_DIM), lambda b, i: (0, 0))],
        out_specs=pl.BlockSpec((1, DIFF_WIDTH, tq), lambda b, i: (b, 0, i)),
        scratch_shapes=[pltpu.VMEM((ns, 1, 2 * tq), F32),
                        pltpu.VMEM((ns, 2 * HEAD_DIM + VT_PAD, 2 * tq), F32),
                        pltpu.VMEM((2, ns, tq, 2 * tq), F32)],
        compiler_params=pltpu.CompilerParams(dimension_semantics=("parallel", "arbitrary"),
                                             vmem_limit_bytes=VMEM_LIMIT),
        name="diff_prompt",
    )(dqt, dka, dvt, dlam)


def _pad_rows(x, rows):
    return jnp.concatenate([x, jnp.zeros((rows - x.shape[0], x.shape[1]), x.dtype)], axis=0)


def _page_pipeline(pt_ref, hbm_refs, bufs, sem, n_pages):
    b, nb = pl.program_id(0), pl.num_programs(0)

    def copies(seq, slot, pg):
        phys = pt_ref[seq * n_pages + pg]
        return [pltpu.make_async_copy(h.at[phys], buf.at[slot, pg], sem.at[k, slot])
                for k, (h, buf) in enumerate(zip(hbm_refs, bufs))]

    def for_pages(seq, slot, start):
        def body(pg, carry):
            for cp in copies(seq, slot, pg):
                cp.start() if start else cp.wait()
            return carry
        lax.fori_loop(0, n_pages, body, 0)

    @pl.when(b == 0)
    def _():
        for_pages(0, 0, True)

    @pl.when(b + 1 < nb)
    def _():
        for_pages(b + 1, (b + 1) % 2, True)

    slot = b % 2
    for_pages(b, slot, False)
    return slot


def _softmax_all(s_ref, p_ref):
    s3 = s_ref[...]
    m = jnp.max(jnp.max(s3, axis=0), axis=-1, keepdims=True)
    p3 = jnp.exp(s3 - m[None])
    p_ref[...] = p3.astype(BF16)
    return jnp.sum(jnp.sum(p3, axis=0), axis=-1, keepdims=True)


def _fox_sample_kernel(pt_ref, qbd_ref, lcol_ref, lrow_ref, knew_ref, vnew_ref, sup_ref, ck_hbm, cv_hbm, clf_hbm,
                       o_ref, kbuf, vbuf, lbuf, r_ref, s_ref, p_ref, sem, *, n_pages, t_new):
    slot = _page_pipeline(pt_ref, (ck_hbm, cv_hbm, clf_hbm), (kbuf, vbuf, lbuf), sem, n_pages)
    nh = FOX_HEADS
    rows = t_new * nh
    page = kbuf.shape[-1]
    qbd = qbd_ref[0]
    lc = lcol_ref[0]
    parts, run = [], None
    for t in range(t_new):
        run = lc[t * nh:(t + 1) * nh] if run is None else run + lc[t * nh:(t + 1) * nh]
        parts.append(run)
    qc = jnp.concatenate(parts, axis=0)

    lp_all = lbuf[slot].reshape(n_pages * nh, page)
    hi, mid, lo = _split3(lp_all)
    sup = sup_ref[...]
    r_in = _dot(hi, sup) + _dot(mid, sup) + _dot(lo, sup)
    tot = r_in[:, 0:1] + lp_all[:, 0:1]
    later, c = [], jnp.zeros((nh, 1), F32)
    for pg in reversed(range(n_pages)):
        later.append(c)
        c = c + tot[pg * nh:(pg + 1) * nh]
    r_ref[...] = r_in + jnp.concatenate(later[::-1], axis=0)

    def qk(pg, carry):
        r_pg = r_ref[pl.ds(pl.multiple_of(pg * nh, nh), nh), :]
        s_ref[pg] = _dot(qbd, kbuf[slot, pg].astype(BF16)) + jnp.concatenate([r_pg] * t_new, axis=0) + qc
        return carry

    lax.fori_loop(0, n_pages, qk, 0, unroll=PAGE_UNROLL)

    kn = _pad_rows(knew_ref[0], page).astype(BF16)
    vn = _pad_rows(vnew_ref[0], page).astype(BF16)
    lr = lrow_ref[0]
    lane8 = lax.broadcasted_iota(jnp.int32, lr.shape, 1)
    qct = jnp.zeros(lr.shape, F32)
    for t in range(t_new):
        qct = qct + jnp.where(lane8 >= t, lr[:, t:t + 1], 0.0)
    rr = lax.broadcasted_iota(jnp.int32, (rows, page), 0)
    cc = lax.broadcasted_iota(jnp.int32, (rows, page), 1)
    s_new = _dot_nt(qbd, kn) + qc - jnp.concatenate([qct] * t_new, axis=0)
    s_ref[n_pages] = jnp.where((cc * nh <= rr) & (cc < t_new), s_new, NEG)

    l = _softmax_all(s_ref, p_ref)

    def pv(pg, acc):
        return acc + _dot_nt(p_ref[pg], vbuf[slot, pg].astype(BF16))

    acc = lax.fori_loop(0, n_pages, pv, _dot(p_ref[n_pages], vn), unroll=PAGE_UNROLL)
    o = acc / l
    o_a, o_b = o[:, :LANES], o[:, LANES:]
    kv = (lax.broadcasted_iota(jnp.int32, (rows, LANES), 0) % nh) // (FOX_HEADS // FOX_KV_HEADS)
    sel = jnp.where(kv == 0, o_a, jnp.where(kv == 1, pltpu.roll(o_a, HEAD_DIM, 1),
                    jnp.where(kv == 2, o_b, pltpu.roll(o_b, HEAD_DIM, 1))))
    o_ref[0] = sel[:, :HEAD_DIM]


def _fox_sample(pt_flat, qbd, lcol, lrow, knew, vnew, ckt, cvt, clf, n_pages):
    db, rows, _ = qbd.shape
    t_new = rows // FOX_HEADS
    page = ckt.shape[-1]
    sup = jnp.asarray(np.tril(np.ones((page, page), np.float32), -1), dtype=BF16)
    per_b = lambda shape: pl.BlockSpec((1,) + shape, lambda b, pt: (b, 0, 0))
    hbm = pl.BlockSpec(memory_space=pl.ANY)
    in_specs = [per_b((rows, MXU_DIM)), per_b((rows, 1)), per_b((FOX_HEADS, LANES)),
                per_b((8, FOX_KV_WIDTH)), per_b((8, FOX_KV_WIDTH)),
                pl.BlockSpec((page, page), lambda b, pt: (0, 0)), hbm, hbm, hbm]
    return pl.pallas_call(
        functools.partial(_fox_sample_kernel, n_pages=n_pages, t_new=t_new),
        out_shape=jax.ShapeDtypeStruct((db, rows, HEAD_DIM), F32),
        grid_spec=pltpu.PrefetchScalarGridSpec(
            num_scalar_prefetch=1,
            grid=(db,),
            in_specs=in_specs,
            out_specs=pl.BlockSpec((1, rows, HEAD_DIM), lambda b, pt: (b, 0, 0)),
            scratch_shapes=[pltpu.VMEM((2, n_pages, FOX_KV_WIDTH, page), F32),
                            pltpu.VMEM((2, n_pages, FOX_KV_WIDTH, page), F32),
                            pltpu.VMEM((2, n_pages, FOX_HEADS, page), F32),
                            pltpu.VMEM((n_pages * FOX_HEADS, page), F32),
                            pltpu.VMEM((n_pages + 1, rows, page), F32),
                            pltpu.VMEM((n_pages + 1, rows, page), BF16),
                            pltpu.SemaphoreType.DMA((3, 2))]),
        compiler_params=pltpu.CompilerParams(dimension_semantics=("arbitrary",),
                                             vmem_limit_bytes=VMEM_LIMIT),
        name="fox_sample",
    )(pt_flat, qbd, lcol, lrow, knew, vnew, sup, ckt, cvt, clf)


def _diff_sample_kernel(pt_ref, qbd_ref, knew_ref, vnew_ref, dl_ref, ck_hbm, cv_hbm,
                        o_ref, kbuf, vbuf, s_ref, p_ref, sem, *, n_pages, t_new, lam_init):
    slot = _page_pipeline(pt_ref, (ck_hbm, cv_hbm), (kbuf, vbuf), sem, n_pages)
    half = t_new * DIFF_HEADS
    rows = 2 * half
    page = kbuf.shape[-1]
    qbd = qbd_ref[0]

    def qk(pg, carry):
        s_ref[pg] = _dot(qbd, kbuf[slot, pg].astype(BF16))
        return carry

    lax.fori_loop(0, n_pages, qk, 0, unroll=PAGE_UNROLL)

    kn = _pad_rows(knew_ref[0], page).astype(BF16)
    vn = _pad_rows(vnew_ref[0], page).astype(BF16)
    rr = lax.broadcasted_iota(jnp.int32, (rows, page), 0)
    cc = lax.broadcasted_iota(jnp.int32, (rows, page), 1)
    t_row = (rr % half) // DIFF_HEADS
    s_ref[n_pages] = jnp.where((cc <= t_row) & (cc < t_new), _dot_nt(qbd, kn), NEG)

    l = _softmax_all(s_ref, p_ref)

    def pv(pg, acc):
        a0, a1 = acc
        pp = p_ref[pg]
        v0 = vbuf[slot, pg, pl.ds(0, page, stride=DIFF_KV_HEADS), :].astype(BF16)
        v1 = vbuf[slot, pg, pl.ds(1, page, stride=DIFF_KV_HEADS), :].astype(BF16)
        return a0 + _dot(pp, v0), a1 + _dot(pp, v1)

    w = 2 * HEAD_DIM
    new = _dot(p_ref[n_pages], vn)
    a0, a1 = lax.fori_loop(0, n_pages, pv, (new[:, :w], new[:, w:]), unroll=PAGE_UNROLL)
    kv = (lax.broadcasted_iota(jnp.int32, (rows, w), 0) % DIFF_HEADS) // (DIFF_HEADS // DIFF_KV_HEADS)
    sel = jnp.where(kv == 0, a0, a1) / l
    o_ref[0] = sel[:half] - _diff_lambda(dl_ref, lam_init) * sel[half:]


def _diff_sample(pt_flat, qbd, knew, vnew, dlam, ckt, cv, n_pages, lam_init):
    db, rows, _ = qbd.shape
    t_new = rows // (2 * DIFF_HEADS)
    page = ckt.shape[-1]
    per_b = lambda shape: pl.BlockSpec((1,) + shape, lambda b, pt: (b, 0, 0))
    hbm = pl.BlockSpec(memory_space=pl.ANY)
    in_specs = [per_b((rows, MXU_DIM)), per_b((8, DIFF_KV_WIDTH)), per_b((8, DIFF_KV_WIDTH)),
                pl.BlockSpec((4, HEAD_DIM), lambda b, pt: (0, 0)), hbm, hbm]
    return pl.pallas_call(
        functools.partial(_diff_sample_kernel, n_pages=n_pages, t_new=t_new, lam_init=lam_init),
        out_shape=jax.ShapeDtypeStruct((db, rows // 2, 2 * HEAD_DIM), F32),
        grid_spec=pltpu.PrefetchScalarGridSpec(
            num_scalar_prefetch=1,
            grid=(db,),
            in_specs=in_specs,
            out_specs=pl.BlockSpec((1, rows // 2, 2 * HEAD_DIM), lambda b, pt: (b, 0, 0)),
            scratch_shapes=[pltpu.VMEM((2, n_pages, DIFF_KV_WIDTH, page), F32),
                            pltpu.VMEM((2, n_pages, DIFF_KV_HEADS * page, 2 * HEAD_DIM), F32),
                            pltpu.VMEM((n_pages + 1, rows, page), F32),
                            pltpu.VMEM((n_pages + 1, rows, page), BF16),
                            pltpu.SemaphoreType.DMA((2, 2))]),
        compiler_params=pltpu.CompilerParams(dimension_semantics=("arbitrary",),
                                             vmem_limit_bytes=VMEM_LIMIT),
        name="diff_sample",
    )(pt_flat, qbd, knew, vnew, dlam, ckt, cv)


def _heads_rmsnorm_t(xt, width, gain):
    parts = []
    for h in range(xt.shape[0] // width):
        xh = xt[h * width:(h + 1) * width]
        parts.append(xh * lax.rsqrt(jnp.mean(xh * xh, axis=0, keepdims=True) + EPS))
    return jnp.concatenate(parts, axis=0) * jnp.concatenate([gain] * (xt.shape[1] // LANES), axis=1)


def _ffn_kernel(*refs, tm, d_ff, post_scale, stateful):
    if stateful:
        _ffn_tile(refs, tm, d_ff, post_scale, stateful)
        return
    carry_ref, meta_ref = refs[-2:]
    b, i = pl.program_id(0), pl.program_id(1)

    @pl.when((i > 0) | (b == 0))
    def _():
        _ffn_tile(refs[:-1], tm, d_ff, post_scale, stateful)

        @pl.when(i == 0)
        def _():
            meta_ref[...] = carry_ref[...]

    @pl.when((i == 0) & (b > 0))
    def _():
        carry_ref[...] = meta_ref[...]


def _ffn_tile(refs, tm, d_ff, post_scale, stateful):
    if stateful:
        (x_ref, fo_ref, do_ref, gm64_ref, gm128_ref, gfo_ref, gdo_ref, wo_ref, gffn_ref, wup_ref,
         cw_ref, cb_ref, wdn_ref, s0_ref, s1_ref, y_ref, h_ref, hs_ref) = refs
        fo = _group_rmsnorm(fo_ref[0], gm64_ref[...], gfo_ref[...])
        do = _group_rmsnorm(do_ref[0], gm128_ref[...], gdo_ref[...]) * post_scale
        merged = jnp.concatenate([fo, do], axis=1)
    else:
        (x_ref, head_ref, fo_ref, do_ref, gfo_ref, gdo_ref, wo_ref, gffn_ref, wup_ref,
         cw_ref, cb_ref, wdn_ref, y_ref, h_ref, hs_ref, carry_ref) = refs

        @pl.when(pl.program_id(1) == 0)
        def _():
            carry_ref[...] = jnp.zeros_like(carry_ref)

        fo = _heads_rmsnorm_t(fo_ref[0], HEAD_DIM, gfo_ref[...])
        do = _heads_rmsnorm_t(do_ref[0], 2 * HEAD_DIM, gdo_ref[...]) * post_scale
        merged = jnp.concatenate([fo, do], axis=0).T

    att = _dot(merged.astype(BF16), wo_ref[...])
    x = x_ref[0]
    if not stateful:
        x = jnp.where(pl.program_id(1) == 0, head_ref[...], x)
    x1 = x + att
    hn = x1 * lax.rsqrt(jnp.mean(x1 * x1, axis=-1, keepdims=True) + EPS) * gffn_ref[...]
    hnb = hn.astype(BF16)
    ch = FF_CHUNK
    n_chunks = d_ff // ch
    rid = lax.broadcasted_iota(jnp.int32, (8, ch), 0)

    def produce(c):
        for half in range(2):
            col = half * d_ff + c * ch
            hs_ref[c % 2, half] = _dot(hnb, wup_ref[:, col:col + ch])

    def consume(c, acc):
        conv = []
        for half in range(2):
            col = half * d_ff + c * ch
            h = hs_ref[c % 2, half]
            if stateful:
                nseq = s0_ref.shape[0]
                hm1 = jnp.concatenate([s1_ref[:, col:col + ch], h[:tm - nseq]], axis=0)
                hm2 = jnp.concatenate([s0_ref[:, col:col + ch], s1_ref[:, col:col + ch], h[:tm - 2 * nseq]], axis=0)
                h_ref[:, col:col + ch] = h[tm - 2 * nseq:]
            else:
                prev = carry_ref[:, col:col + ch]
                r1, r2 = pltpu.roll(h, 1, 0), pltpu.roll(h, 2, 0)
                top1 = jnp.where(rid < 1, pltpu.roll(prev, 1, 0), r1[0:8])
                top2 = jnp.where(rid < 2, pltpu.roll(prev, 2, 0), r2[0:8])
                hm1 = jnp.concatenate([top1, r1[8:]], axis=0)
                hm2 = jnp.concatenate([top2, r2[8:]], axis=0)
                carry_ref[:, col:col + ch] = h[tm - 8:]
                h_ref[0, :, col:col + ch] = h[tm - 8:]
            conv.append(cb_ref[:, col:col + ch] + cw_ref[0:1, col:col + ch] * hm2
                        + cw_ref[1:2, col:col + ch] * hm1 + cw_ref[2:3, col:col + ch] * h)
        g, u = conv
        act = g * (1.0 / (1.0 + jnp.exp(-g))) * u
        return acc + _dot(act.astype(BF16), wdn_ref[c * ch:(c + 1) * ch, :])

    acc = x1
    produce(0)
    for c in range(n_chunks):
        if c + 1 < n_chunks:
            produce(c + 1)
        acc = consume(c, acc)
    y_ref[0] = acc


def _merge_ffn(x, fo, do, p, tm, state=None, head=None):
    nb, lt, d = x.shape
    d_ff = p["w_down"].shape[0]
    stateful = state is not None
    single = pl.Buffered(1)
    const = lambda shape: pl.BlockSpec(shape, lambda b, i: (0,) * len(shape), pipeline_mode=single)
    weights = [const((FOX_WIDTH + DIFF_WIDTH, d)), const((1, d)), const((d, 2 * d_ff)), const((CONV_W, 2 * d_ff)),
               const((1, 2 * d_ff)), const((d_ff, d))]
    wargs = [p["w_o"], p["g_ffn"], p["w_up"], p["conv_w"], p["conv_b"], p["w_down"]]
    if stateful:
        nseq = state[0].shape[0]
        row = lambda w: pl.BlockSpec((1, tm, w), lambda b, i: (b, i, 0), pipeline_mode=single)
        in_specs = ([row(d), row(FOX_WIDTH), row(DIFF_WIDTH), const((MXU_DIM, MXU_DIM)), const((MXU_DIM, MXU_DIM)),
                     const((1, FOX_WIDTH)), const((1, DIFF_WIDTH))] + weights
                    + [const((nseq, 2 * d_ff)), const((nseq, 2 * d_ff))])
        args = [x, fo, do, p["gm64"], p["gm128"], p["g_fo"], p["g_do"]] + wargs + list(state)
        h_shape = jax.ShapeDtypeStruct((2 * nseq, 2 * d_ff), F32)
        h_spec = pl.BlockSpec((2 * nseq, 2 * d_ff), lambda b, i: (0, 0))
        scratch = [pltpu.VMEM((2, 2, tm, FF_CHUNK), F32)]
    else:
        real = pl.BlockSpec((1, tm, d), lambda b, i: (b, jnp.maximum(i - 1, 0), 0))
        in_specs = ([real, const((tm, d)),
                     pl.BlockSpec((1, FOX_WIDTH, tm), lambda b, i: (b, 0, i)),
                     pl.BlockSpec((1, DIFF_WIDTH, tm), lambda b, i: (b, 0, i)),
                     const((FOX_WIDTH, LANES)), const((DIFF_WIDTH, LANES))] + weights)
        args = [x, head, fo, do, p["g_fo_t"], p["g_do_t"]] + wargs
        h_shape = jax.ShapeDtypeStruct((nb, 8, 2 * d_ff), F32)
        h_spec = pl.BlockSpec((1, 8, 2 * d_ff), lambda b, i: (b, 0, 0))
        scratch = [pltpu.VMEM((2, 2, tm, FF_CHUNK), F32), pltpu.VMEM((8, 2 * d_ff), F32),
                   pltpu.VMEM((8, 2 * d_ff), F32)]
    y_spec = pl.BlockSpec((1, tm, d), lambda b, i: (b, i, 0)) if stateful else real
    n_tiles = lt // tm + (0 if stateful else 1)
    kern = functools.partial(_ffn_kernel, tm=tm, d_ff=d_ff, post_scale=p["post_scale"], stateful=stateful)
    return pl.pallas_call(
        kern,
        out_shape=[jax.ShapeDtypeStruct((nb, lt, d), F32), h_shape],
        grid=(nb, n_tiles),
        in_specs=in_specs,
        out_specs=[y_spec, h_spec],
        scratch_shapes=scratch,
        compiler_params=pltpu.CompilerParams(dimension_semantics=("arbitrary", "arbitrary"),
                                             vmem_limit_bytes=VMEM_LIMIT),
        name="merge_ffn_sample" if stateful else "merge_ffn_prompt",
    )(*args)


def kernel(x_prompt, x_sample, cache_fox_k, cache_fox_v, cache_fox_logf, cache_diff_k, cache_diff_v, state_ffn_conv, page_table, meta_tokens, attn_norm_g, w_in, b_f, fox_qn_g, fox_kn_g, fox_on_g, diff_qn_g, diff_kn_g, diff_lambda, diff_subln_g, w_o, ffn_norm_g, w_up, conv_w, conv_b, w_down):
    nb, seq, d = x_prompt.shape
    db, t_new, _ = x_sample.shape
    depth, n_phys, page = cache_fox_k.shape[:3]
    n_pages = page_table.shape[1]
    assert depth == 1, "single layer only"
    layer = 0
    lam_init = 0.8 - 0.6 * math.exp(-0.3 * layer)
    past_len = n_pages * page

    wi = w_in[layer]
    o = np.cumsum([0, FOX_WIDTH, FOX_KV_WIDTH, FOX_KV_WIDTH, FOX_HEADS, DIFF_WIDTH, DIFF_KV_WIDTH, DIFF_KV_WIDTH])
    seg = lambda k: wi[:, o[k]:o[k + 1]]
    w_all = jnp.concatenate([seg(0), seg(1), seg(2), seg(4), seg(5), seg(6),
                             jnp.pad(seg(3), ((0, 0), (0, LANES - FOX_HEADS)))], axis=1).astype(BF16)
    g_fo = fox_on_g[layer].reshape(1, FOX_WIDTH)
    g_do = jnp.tile(diff_subln_g[layer], DIFF_HEADS)[None, :]
    lane_rep = lambda g: jnp.broadcast_to(g.reshape(-1, 1), (g.shape[-1], LANES))
    p = {
        "g_attn": attn_norm_g[layer][None, :],
        "w_in": w_all,
        "gm64": _block_diag_mean(HEAD_DIM),
        "gm128": _block_diag_mean(2 * HEAD_DIM),
        "g_heads": jnp.concatenate([jnp.tile(fox_qn_g[layer], FOX_HEADS), jnp.tile(fox_kn_g[layer], FOX_KV_HEADS),
                                    jnp.tile(diff_qn_g[layer], 2 * DIFF_HEADS),
                                    jnp.tile(diff_kn_g[layer], 2 * DIFF_KV_HEADS)])[None, :],
        "b_f": jnp.pad(b_f[layer], (0, LANES - FOX_HEADS))[None, :],
        "g_fo": g_fo, "g_do": g_do, "g_fo_t": lane_rep(g_fo), "g_do_t": lane_rep(g_do),
        "post_scale": 1.0 - lam_init,
        "w_o": w_o[layer].astype(BF16),
        "g_ffn": ffn_norm_g[layer][None, :],
        "w_up": w_up[layer].astype(BF16),
        "conv_w": conv_w[layer],
        "conv_b": conv_b[layer][None, :],
        "w_down": w_down[layer].astype(BF16),
    }
    dlam = diff_lambda[layer]

    tm = ROW_TILE
    pad = (-N_META) % tm
    lp = pad + N_META + seq
    assert pad + N_META == tm and seq % tm == 0
    head = jnp.concatenate([jnp.zeros((pad, d), x_prompt.dtype), meta_tokens.astype(x_prompt.dtype)], axis=0)
    pos_p = jnp.arange(lp) - pad
    fk, fv, lf, dk, dv, qat, ka, vt, dqt, dka, dvt = _project(x_prompt, pos_p, tm, p, head=head)
    fo_t = _fox_prompt(qat, ka, vt, tm, pad)
    do_t = _diff_prompt(dqt, dka, dvt, dlam, tm, pad, lam_init)
    y_prompt, hlast = _merge_ffn(x_prompt, fo_t, do_t, p, tm, head=head)
    ltot = N_META + seq
    pk = fk[:, pad:].reshape(1, nb, ltot, FOX_KV_HEADS, HEAD_DIM)
    pv = fv[:, pad:].reshape(1, nb, ltot, FOX_KV_HEADS, HEAD_DIM)
    pf = lf[:, pad:][None]
    pdk = dk[:, pad:].reshape(1, nb, ltot, DIFF_KV_HEADS, 2, HEAD_DIM)
    pdv = dv[:, pad:].reshape(1, nb, ltot, DIFF_KV_HEADS, 2 * HEAD_DIM)
    pc = hlast[:, 8 - (CONV_W - 1):][None]

    rows = db * t_new
    pos_s = past_len + (jnp.arange(rows) % t_new)
    qf, fk, fv, lf, dqb, dk, dv = _project(x_sample.reshape(1, rows, d), pos_s, min(ROW_TILE, rows), p)
    g_f = FOX_HEADS // FOX_KV_HEADS
    g_d = DIFF_HEADS // DIFF_KV_HEADS
    eye_f = jnp.eye(FOX_KV_HEADS, dtype=BF16)
    qbd_f = jnp.einsum('btkgd,kj->btkgjd', qf.reshape(db, t_new, FOX_KV_HEADS, g_f, HEAD_DIM), eye_f)
    qbd_f = qbd_f.reshape(db, t_new * FOX_HEADS, FOX_KV_WIDTH)
    eye_d, eye_2 = jnp.eye(DIFF_KV_HEADS, dtype=BF16), jnp.eye(2, dtype=BF16)
    qbd_d = jnp.einsum('btkgwd,kj,wv->bwtkgjvd', dqb.reshape(db, t_new, DIFF_KV_HEADS, g_d, 2, HEAD_DIM), eye_d, eye_2)
    qbd_d = qbd_d.reshape(db, 2 * t_new * DIFF_HEADS, DIFF_KV_WIDTH)
    lf_s = lf.reshape(db, t_new, FOX_HEADS)
    lcol = lf_s.reshape(db, t_new * FOX_HEADS, 1)
    lrow = jnp.pad(jnp.swapaxes(lf_s, 1, 2), ((0, 0), (0, 0), (0, LANES - t_new)))
    new_rows = lambda a: jnp.pad(a.reshape(db, t_new, a.shape[-1]), ((0, 0), (0, 8 - t_new), (0, 0)))
    pt_flat = page_table.reshape(-1).astype(jnp.int32)
    ckt = jnp.transpose(cache_fox_k[layer], (0, 2, 3, 1)).reshape(n_phys, FOX_KV_WIDTH, page)
    cvt = jnp.transpose(cache_fox_v[layer], (0, 2, 3, 1)).reshape(n_phys, FOX_KV_WIDTH, page)
    clf = jnp.swapaxes(cache_fox_logf[layer], 1, 2)
    fo_s = _fox_sample(pt_flat, qbd_f, lcol, lrow, new_rows(fk), new_rows(fv), ckt, cvt, clf, n_pages)
    cdkt = jnp.transpose(cache_diff_k[layer], (0, 2, 3, 4, 1)).reshape(n_phys, DIFF_KV_WIDTH, page)
    cdv = cache_diff_v[layer].reshape(n_phys, page * DIFF_KV_HEADS, 2 * HEAD_DIM)
    do_s = _diff_sample(pt_flat, qbd_d, new_rows(dk), new_rows(dv), dlam, cdkt, cdv, n_pages, lam_init)

    tmajor = lambda a, w: jnp.swapaxes(a.reshape(db, t_new, w), 0, 1).reshape(1, rows, w)
    st = state_ffn_conv[layer]
    ys, hs = _merge_ffn(tmajor(x_sample, d), tmajor(fo_s, FOX_WIDTH), tmajor(do_s, DIFF_WIDTH), p, rows,
                        state=(st[:, 0], st[:, 1]))
    y_sample = jnp.swapaxes(ys.reshape(t_new, db, d), 0, 1)
    sc = jnp.swapaxes(hs.reshape(CONV_W - 1, db, hs.shape[-1]), 0, 1)[None]
    sk = fk.reshape(1, db, t_new, FOX_KV_HEADS, HEAD_DIM)
    sv = fv.reshape(1, db, t_new, FOX_KV_HEADS, HEAD_DIM)
    sf = lf_s[None]
    sdk = dk.reshape(1, db, t_new, DIFF_KV_HEADS, 2, HEAD_DIM)
    sdv = dv.reshape(1, db, t_new, DIFF_KV_HEADS, 2 * HEAD_DIM)
    return (y_prompt, y_sample, pk, pv, pf, pdk, pdv, pc, sk, sv, sf, sdk, sdv, sc)
```

```python
import functools
import math

import numpy as np
import jax
import jax.numpy as jnp
from jax import lax
from jax.experimental import pallas as pl
from jax.experimental.pallas import tpu as pltpu

F32 = jnp.float32
BF16 = jnp.bfloat16

HEAD_DIM = 64
FOX_HEADS = 8
FOX_KV_HEADS = 4
DIFF_HEADS = 4
DIFF_KV_HEADS = 2
FOX_WIDTH = FOX_HEADS * HEAD_DIM
FOX_KV_WIDTH = FOX_KV_HEADS * HEAD_DIM
DIFF_WIDTH = DIFF_HEADS * 2 * HEAD_DIM
DIFF_KV_WIDTH = DIFF_KV_HEADS * 2 * HEAD_DIM
ROT_DIM = HEAD_DIM // 4
ROPE_THETA = 500000.0
CONV_W = 3
N_META = 16
EPS = 1e-6
NEG = -1e30
SCALE = HEAD_DIM ** -0.5
LOG2E = 1.4426950408889634

LANES = 128
MXU_DIM = 256
ROW_TILE = 256
FF_CHUNK = 256
PAGE_UNROLL = 16
VMEM_LIMIT = 56 * 1024 * 1024
VT_PAD = 16
FOX_VT_ROWS = FOX_KV_HEADS * (HEAD_DIM + VT_PAD)
DIFF_VT_ROWS = DIFF_KV_HEADS * (2 * HEAD_DIM + VT_PAD)

_FQ, _FK, _FV, _DQ, _DK, _DV, _FF, _W_TOTAL = 0, 512, 768, 1024, 1536, 1792, 2048, 2176
_AUG_C, _AUG_G0 = HEAD_DIM, HEAD_DIM + 3


def _split3(x):
    hi = x.astype(BF16)
    r = x - hi.astype(F32)
    mid = r.astype(BF16)
    lo = (r - mid.astype(F32)).astype(BF16)
    return hi, mid, lo


def _dot(a, b):
    return jnp.dot(a, b, preferred_element_type=F32)


def _dot_nt(a, b):
    return lax.dot_general(a, b, (((1,), (1,)), ((), ())), preferred_element_type=F32)


def _group_mean_sq(x, gmat):
    outs = []
    for c in range(x.shape[1] // MXU_DIM):
        s = x[:, c * MXU_DIM:(c + 1) * MXU_DIM]
        outs.append(_dot((s * s).astype(BF16), gmat))
    return outs[0] if len(outs) == 1 else jnp.concatenate(outs, axis=1)


def _group_rmsnorm(x, gmat, gain):
    return x * lax.rsqrt(_group_mean_sq(x, gmat) + EPS) * gain


def _proj_kernel(*refs, prompt):
    (x_ref, gattn_ref, w_ref, gm_ref, gh_ref, bf_ref, cos_ref, sa_ref, sb_ref) = refs[:9]
    if prompt:
        (tri_ref, pq_ref, pk_ref, oq_ref, ok_ref, head_ref,
         fk_ref, fv_ref, lf_ref, dk_ref, dv_ref,
         qat_ref, ka_ref, vt_ref, dqt_ref, dka_ref, dvt_ref, carry_ref) = refs[9:]
    else:
        (qf_ref, fk_ref, fv_ref, lf_ref, dqb_ref, dk_ref, dv_ref) = refs[9:]
    x = x_ref[0]
    if prompt:
        x = jnp.where(pl.program_id(1) == 0, head_ref[...], x)
    xn = x * lax.rsqrt(jnp.mean(x * x, axis=-1, keepdims=True) + EPS) * gattn_ref[...]
    y = _dot(xn.astype(BF16), w_ref[...])
    gm = gm_ref[...]
    cos_t, sin_a, sin_b = cos_ref[...], sa_ref[...], sb_ref[...]

    def rope(seg):
        outs = []
        for c in range(seg.shape[1] // LANES):
            s = seg[:, c * LANES:(c + 1) * LANES]
            outs.append(s * cos_t + pltpu.roll(s, LANES - ROT_DIM // 2, 1) * sin_a
                        + pltpu.roll(s, ROT_DIM // 2, 1) * sin_b)
        return jnp.concatenate(outs, axis=1)

    fq = _group_rmsnorm(y[:, _FQ:_FK], gm, gh_ref[:, 0:512])
    fk = _group_rmsnorm(y[:, _FK:_FV], gm, gh_ref[:, 512:768])
    fv = y[:, _FV:_DQ]
    dq = rope(_group_rmsnorm(y[:, _DQ:_DK], gm, gh_ref[:, 768:1280]))
    dk = rope(_group_rmsnorm(y[:, _DK:_DV], gm, gh_ref[:, 1280:1536]))
    dv = y[:, _DV:_FF]
    z = y[:, _FF:_W_TOTAL] + bf_ref[...]
    lf = jnp.minimum(z, 0.0) - jnp.log1p(jnp.exp(-jnp.abs(z)))
    fk_ref[0] = fk
    fv_ref[0] = fv
    dk_ref[0] = dk
    dv_ref[0] = dv
    lf_ref[0] = lf[:, :FOX_HEADS]
    if not prompt:
        qf_ref[0] = (fq * SCALE).astype(BF16)
        dqb_ref[0] = (dq * SCALE).astype(BF16)
        return

    lane = lax.broadcasted_iota(jnp.int32, lf.shape, 1)
    lf = jnp.where(lane < FOX_HEADS, lf, 0.0)
    hi, mid, lo = _split3(lf)
    tri = tri_ref[...]
    cs = _dot(tri, hi) + _dot(tri, mid) + _dot(tri, lo)

    @pl.when(pl.program_id(1) == 0)
    def _():
        carry_ref[...] = jnp.zeros_like(carry_ref)

    c = cs + carry_ref[0:1, :]
    tm = c.shape[0]
    carry_ref[0:1, :] = c[tm - 1:tm, :]

    x3 = jnp.concatenate(_split3(c * LOG2E), axis=1)
    eq = _dot(x3, pq_ref[...]) + oq_ref[...]
    ek = _dot(x3, pk_ref[...]) + ok_ref[...]
    low = lane < HEAD_DIM
    half = lambda chunk, odd: pltpu.roll(chunk, HEAD_DIM, 1) if odd else chunk
    fqs = fq * (SCALE * LOG2E)
    for h in range(FOX_HEADS):
        chunk = fqs[:, (h // 2) * LANES:(h // 2 + 1) * LANES]
        qa = jnp.where(low, half(chunk, h % 2), eq[:, h * LANES:(h + 1) * LANES])
        qat_ref[0, h * LANES:(h + 1) * LANES, :] = qa.T.astype(BF16)
    for h in range(FOX_KV_HEADS):
        chunk = fk[:, (h // 2) * LANES:(h // 2 + 1) * LANES]
        ka = jnp.where(low, half(chunk, h % 2), ek[:, h * LANES:(h + 1) * LANES])
        ka_ref[0, :, h * LANES:(h + 1) * LANES] = ka.astype(BF16)
    ones_rows = jnp.where(lax.broadcasted_iota(jnp.int32, (VT_PAD, tm), 0) == 0, 1.0, 0.0).astype(BF16)
    fw = HEAD_DIM + VT_PAD
    for c2 in range(FOX_KV_WIDTH // LANES):
        t = fv[:, c2 * LANES:(c2 + 1) * LANES].T.astype(BF16)
        for k in range(LANES // HEAD_DIM):
            h = c2 * (LANES // HEAD_DIM) + k
            vt_ref[0, h * fw:h * fw + HEAD_DIM, :] = t[k * HEAD_DIM:(k + 1) * HEAD_DIM]
            vt_ref[0, h * fw + HEAD_DIM:(h + 1) * fw, :] = ones_rows
    dqs = dq * (SCALE * LOG2E)
    for c2 in range(DIFF_WIDTH // LANES):
        dqt_ref[0, c2 * LANES:(c2 + 1) * LANES, :] = dqs[:, c2 * LANES:(c2 + 1) * LANES].T.astype(BF16)
    for h in range(2 * DIFF_KV_HEADS):
        chunk = dk[:, (h // 2) * LANES:(h // 2 + 1) * LANES]
        dka_ref[0, :, h * LANES:(h + 1) * LANES] = jnp.where(low, half(chunk, h % 2), 0.0).astype(BF16)
    dw = 2 * HEAD_DIM + VT_PAD
    for h in range(DIFF_KV_HEADS):
        dvt_ref[0, h * dw:h * dw + 2 * HEAD_DIM, :] = dv[:, h * LANES:(h + 1) * LANES].T.astype(BF16)
        dvt_ref[0, h * dw + 2 * HEAD_DIM:(h + 1) * dw, :] = ones_rows


def _rope_tables(pos):
    inv_freq = ROPE_THETA ** (-jnp.arange(0, ROT_DIM, 2, dtype=F32) / ROT_DIM)
    ang = pos.astype(F32)[:, None] * inv_freq[None, :]
    cos, sin = jnp.cos(ang), jnp.sin(ang)
    half = ROT_DIM // 2
    ones = jnp.ones((pos.shape[0], HEAD_DIM - ROT_DIM), F32)
    zeros_h = jnp.zeros((pos.shape[0], half), F32)
    zeros_r = jnp.zeros((pos.shape[0], HEAD_DIM - ROT_DIM), F32)
    cos_h = jnp.concatenate([cos, cos, ones], axis=1)
    sa_h = jnp.concatenate([-sin, zeros_h, zeros_r], axis=1)
    sb_h = jnp.concatenate([zeros_h, sin, zeros_r], axis=1)
    rep = LANES // HEAD_DIM
    return jnp.tile(cos_h, (1, rep)), jnp.tile(sa_h, (1, rep)), jnp.tile(sb_h, (1, rep))


def _block_diag_mean(group):
    idx = np.arange(MXU_DIM) // group
    return jnp.asarray((idx[:, None] == idx[None, :]).astype(np.float32) / group, dtype=BF16)


def _bias_placement():
    g = FOX_HEADS // FOX_KV_HEADS
    pq = np.zeros((3 * LANES, FOX_HEADS * LANES), np.float32)
    pk = np.zeros((3 * LANES, FOX_KV_HEADS * LANES), np.float32)
    oq = np.zeros((1, FOX_HEADS * LANES), np.float32)
    ok = np.zeros((1, FOX_KV_HEADS * LANES), np.float32)
    for h in range(FOX_HEADS):
        kvh, gi = divmod(h, g)
        for piece in range(3):
            pq[piece * LANES + h, h * LANES + _AUG_C + piece] = 1.0
            pk[piece * LANES + h, kvh * LANES + _AUG_G0 + 3 * gi + piece] = -1.0
            oq[0, h * LANES + _AUG_G0 + 3 * gi + piece] = 1.0
            ok[0, kvh * LANES + _AUG_C + piece] = 1.0
    return jnp.asarray(pq, BF16), jnp.asarray(pk, BF16), jnp.asarray(oq), jnp.asarray(ok)


def _project(x, pos, tm, p, head=None):
    prompt = head is not None
    nb, lt, d = x.shape
    lt += tm if prompt else 0
    nt = lt // tm
    cos_t, sin_a, sin_b = _rope_tables(pos)
    row = lambda w, dt=None: pl.BlockSpec((1, tm, w), lambda b, i: (b, i, 0))
    col = lambda w: pl.BlockSpec((1, w, tm), lambda b, i: (b, 0, i))
    const = lambda shape: pl.BlockSpec(shape, lambda b, i: (0,) * len(shape))
    tab = pl.BlockSpec((tm, LANES), lambda b, i: (i, 0))
    sds = jax.ShapeDtypeStruct
    x_spec = pl.BlockSpec((1, tm, d), lambda b, i: (b, jnp.maximum(i - 1, 0), 0)) if prompt else row(d)
    in_specs = [x_spec, const((1, d)), const((d, _W_TOTAL)), const((MXU_DIM, MXU_DIM)),
                const((1, 1536)), const((1, LANES)), tab, tab, tab]
    args = [x, p["g_attn"], p["w_in"], p["gm64"], p["g_heads"], p["b_f"], cos_t, sin_a, sin_b]
    f32_rows = [sds((nb, lt, FOX_KV_WIDTH), F32), sds((nb, lt, FOX_KV_WIDTH), F32), sds((nb, lt, FOX_HEADS), F32),
                sds((nb, lt, DIFF_KV_WIDTH), F32), sds((nb, lt, DIFF_KV_WIDTH), F32)]
    f32_specs = [row(FOX_KV_WIDTH), row(FOX_KV_WIDTH), row(FOX_HEADS), row(DIFF_KV_WIDTH), row(DIFF_KV_WIDTH)]
    if prompt:
        tri = jnp.asarray(np.tril(np.ones((tm, tm), np.float32)), dtype=BF16)
        pq, pk, oq, ok = _bias_placement()
        in_specs += [const((tm, tm)), const(pq.shape), const(pk.shape), const(oq.shape), const(ok.shape),
                     const((tm, d))]
        args += [tri, pq, pk, oq, ok, head]
        out_shape = f32_rows + [sds((nb, FOX_HEADS * LANES, lt), BF16), sds((nb, lt, FOX_KV_HEADS * LANES), BF16),
                                sds((nb, FOX_VT_ROWS, lt), BF16), sds((nb, DIFF_WIDTH, lt), BF16),
                                sds((nb, lt, 2 * DIFF_KV_HEADS * LANES), BF16), sds((nb, DIFF_VT_ROWS, lt), BF16)]
        out_specs = f32_specs + [col(FOX_HEADS * LANES), row(FOX_KV_HEADS * LANES), col(FOX_VT_ROWS),
                                 col(DIFF_WIDTH), row(2 * DIFF_KV_HEADS * LANES), col(DIFF_VT_ROWS)]
        scratch = [pltpu.VMEM((8, LANES), F32)]
    else:
        out_shape = ([sds((nb, lt, FOX_WIDTH), BF16)] + f32_rows[:3] + [sds((nb, lt, DIFF_WIDTH), BF16)] + f32_rows[3:])
        out_specs = [row(FOX_WIDTH)] + f32_specs[:3] + [row(DIFF_WIDTH)] + f32_specs[3:]
        scratch = []
    return pl.pallas_call(
        functools.partial(_proj_kernel, prompt=prompt),
        out_shape=out_shape,
        grid=(nb, nt),
        in_specs=in_specs,
        out_specs=out_specs,
        scratch_shapes=scratch,
        compiler_params=pltpu.CompilerParams(dimension_semantics=("parallel", "arbitrary"),
                                             vmem_limit_bytes=VMEM_LIMIT),
        name="proj_prompt" if prompt else "proj_sample",
    )(*args)


def _reset(m_ref, acc_ref):
    m_ref[...] = jnp.full(m_ref.shape, NEG, F32)
    acc_ref[...] = jnp.zeros(acc_ref.shape, F32)


def _flash_block_t(st, vt, m_ref, acc_ref):
    m_prev = m_ref[...]
    m_new = jnp.maximum(m_prev, jnp.max(st, axis=0, keepdims=True))
    alpha = jnp.exp2(m_prev - m_new)
    p = jnp.exp2(st - m_new)
    acc_ref[...] = alpha * acc_ref[...] + _dot(vt, p.astype(BF16))
    m_ref[...] = m_new


def _normalised(acc, dims):
    return acc[:dims] / acc[dims:dims + 1]


def _causal_sweep(i, produce, consume):
    produce(0, 0)

    @pl.when(i == 0)
    def _():
        consume(0, 0, "both")

    @pl.when(i > 0)
    def _():
        produce(1, 1)
        consume(0, 0, "pad")

        def body(jj, carry):
            j = 2 * jj + 1
            produce(j + 1, 0)
            consume(j, 1, False)

            @pl.when(j + 1 < i)
            def _():
                produce(j + 2, 1)
                consume(j + 1, 0, False)
            return carry

        lax.fori_loop(0, i // 2, body, 0)

        @pl.when(i % 2 == 1)
        def _():
            consume(i, 1, "causal")

        @pl.when(i % 2 == 0)
        def _():
            consume(i, 0, "causal")


def _key_mask(kind, tq, nq, pad):
    krow = lax.broadcasted_iota(jnp.int32, (tq, 1), 0)
    if kind == "pad":
        return krow >= pad
    lane = lax.broadcasted_iota(jnp.int32, (1, nq), 1)
    causal = krow <= jnp.where(lane >= tq, lane - tq, lane)
    return causal if kind == "causal" else causal & (krow >= pad)


def _once_for_head_tile(o_ref, tile_fn):
    b, i = pl.program_id(0), pl.program_id(1)

    @pl.when((i > 0) | (b == 0))
    def _():
        tile_fn()

    @pl.when((i == 0) & (b > 0))
    def _():
        o_ref[...] = jnp.zeros(o_ref.shape, o_ref.dtype)


def _fox_prompt_kernel(qt_ref, ka_ref, vt_ref, o_ref, *scratch, tq, pad):
    _once_for_head_tile(o_ref, lambda: _fox_prompt_tile(qt_ref, ka_ref, vt_ref, o_ref, *scratch, tq=tq, pad=pad))


def _fox_prompt_tile(qt_ref, ka_ref, vt_ref, o_ref, m_ref, acc_ref, s_ref, *, tq, pad):
    i = pl.program_id(1)
    hd = HEAD_DIM
    vw = HEAD_DIM + VT_PAD
    _reset(m_ref, acc_ref)

    def produce(j, slot):
        ks = pl.multiple_of(j * tq, tq)
        for kvh in range(FOX_KV_HEADS):
            h0, h1 = 2 * kvh, 2 * kvh + 1
            qt = jnp.concatenate([qt_ref[0, h0 * LANES:(h0 + 1) * LANES, :],
                                  qt_ref[0, h1 * LANES:(h1 + 1) * LANES, :]], axis=1)
            s_ref[slot, kvh] = _dot(ka_ref[0, pl.ds(ks, tq), kvh * LANES:(kvh + 1) * LANES], qt)

    def consume(j, slot, masked):
        ks = pl.multiple_of(j * tq, tq)
        mask = _key_mask(masked, tq, 2 * tq, pad) if masked else None
        for kvh in range(FOX_KV_HEADS):
            st = s_ref[slot, kvh]
            if masked:
                st = jnp.where(mask, st, NEG)
            _flash_block_t(st, vt_ref[0, kvh * vw:(kvh + 1) * vw, pl.ds(ks, tq)], m_ref.at[kvh], acc_ref.at[kvh])

    _causal_sweep(i, produce, consume)
    for kvh in range(FOX_KV_HEADS):
        h0, h1 = 2 * kvh, 2 * kvh + 1
        o = _normalised(acc_ref[kvh], hd)
        o_ref[0, h0 * hd:(h0 + 1) * hd, :] = o[:, :tq]
        o_ref[0, h1 * hd:(h1 + 1) * hd, :] = o[:, tq:]


def _fox_prompt(qat, ka, vt, tq, pad):
    nb, _, lp = qat.shape
    return pl.pallas_call(
        functools.partial(_fox_prompt_kernel, tq=tq, pad=pad),
        out_shape=jax.ShapeDtypeStruct((nb, FOX_WIDTH, lp), F32),
        grid=(nb, lp // tq),
        in_specs=[pl.BlockSpec((1, FOX_HEADS * LANES, tq), lambda b, i: (b, 0, i)),
                  pl.BlockSpec((1, lp, FOX_KV_HEADS * LANES), lambda b, i: (b, 0, 0)),
                  pl.BlockSpec((1, FOX_VT_ROWS, lp), lambda b, i: (b, 0, 0))],
        out_specs=pl.BlockSpec((1, FOX_WIDTH, tq), lambda b, i: (b, 0, i)),
        scratch_shapes=[pltpu.VMEM((FOX_KV_HEADS, 1, 2 * tq), F32),
                        pltpu.VMEM((FOX_KV_HEADS, HEAD_DIM + VT_PAD, 2 * tq), F32),
                        pltpu.VMEM((2, FOX_KV_HEADS, tq, 2 * tq), F32)],
        compiler_params=pltpu.CompilerParams(dimension_semantics=("parallel", "arbitrary"),
                                             vmem_limit_bytes=VMEM_LIMIT),
        name="fox_prompt",
    )(qat, ka, vt)


def _diff_lambda(dl_ref, lam_init):
    dl = dl_ref[...]
    a = jnp.sum(dl[0:1] * dl[1:2], axis=-1, keepdims=True)
    b = jnp.sum(dl[2:3] * dl[3:4], axis=-1, keepdims=True)
    return jnp.exp(a) - jnp.exp(b) + lam_init


def _diff_prompt_kernel(qt_ref, ka_ref, vt_ref, dl_ref, o_ref, *scratch, tq, pad, lam_init):
    _once_for_head_tile(o_ref, lambda: _diff_prompt_tile(qt_ref, ka_ref, vt_ref, dl_ref, o_ref, *scratch,
                                                          tq=tq, pad=pad, lam_init=lam_init))


def _diff_prompt_tile(qt_ref, ka_ref, vt_ref, dl_ref, o_ref, m_ref, acc_ref, s_ref, *, tq, pad, lam_init):
    i = pl.program_id(1)
    hd = HEAD_DIM
    vw = 2 * HEAD_DIM + VT_PAD
    _reset(m_ref, acc_ref)
    streams = [(kvh, w) for kvh in range(DIFF_KV_HEADS) for w in range(2)]

    def produce(j, slot):
        ks = pl.multiple_of(j * tq, tq)
        for n, (kvh, w) in enumerate(streams):
            r0 = (kvh * 4 + w) * hd
            qt = jnp.concatenate([qt_ref[0, r0:r0 + hd, :], qt_ref[0, r0 + 2 * hd:r0 + 3 * hd, :]], axis=1)
            s_ref[slot, n] = _dot(ka_ref[0, pl.ds(ks, tq), n * LANES:n * LANES + hd], qt)

    def consume(j, slot, masked):
        ks = pl.multiple_of(j * tq, tq)
        mask = _key_mask(masked, tq, 2 * tq, pad) if masked else None
        for n, (kvh, w) in enumerate(streams):
            st = s_ref[slot, n]
            if masked:
                st = jnp.where(mask, st, NEG)
            _flash_block_t(st, vt_ref[0, kvh * vw:(kvh + 1) * vw, pl.ds(ks, tq)], m_ref.at[n], acc_ref.at[n])

    _causal_sweep(i, produce, consume)
    lam = _diff_lambda(dl_ref, lam_init)
    w = 2 * hd
    for kvh in range(DIFF_KV_HEADS):
        n1, n2 = 2 * kvh, 2 * kvh + 1
        o = _normalised(acc_ref[n1], w) - lam * _normalised(acc_ref[n2], w)
        o_ref[0, (2 * kvh) * w:(2 * kvh + 1) * w, :] = o[:, :tq]
        o_ref[0, (2 * kvh + 1) * w:(2 * kvh + 2) * w, :] = o[:, tq:]


def _diff_prompt(dqt, dka, dvt, dlam, tq, pad, lam_init):
    nb, _, lp = dqt.shape
    ns = 2 * DIFF_KV_HEADS
    return pl.pallas_call(
        functools.partial(_diff_prompt_kernel, tq=tq, pad=pad, lam_init=lam_init),
        out_shape=jax.ShapeDtypeStruct((nb, DIFF_WIDTH, lp), F32),
        grid=(nb, lp // tq),
        in_specs=[pl.BlockSpec((1, DIFF_WIDTH, tq), lambda b, i: (b, 0, i)),
                  pl.BlockSpec((1, lp, 2 * DIFF_KV_HEADS * LANES), lambda b, i: (b, 0, 0)),
                  pl.BlockSpec((1, DIFF_VT_ROWS, lp), lambda b, i: (b, 0, 0)),
                  pl.BlockSpec((4, HEAD_DIM), lambda b, i: (0, 0))],
        out_specs=pl.BlockSpec((1, DIFF_WIDTH, tq), lambda b, i: (b, 0, i)),
        scratch_shapes=[pltpu.VMEM((ns, 1, 2 * tq), F32),
                        pltpu.VMEM((ns, 2 * HEAD_DIM + VT_PAD, 2 * tq), F32),
                        pltpu.VMEM((2, ns, tq, 2 * tq), F32)],
        compiler_params=pltpu.CompilerParams(dimension_semantics=("parallel", "arbitrary"),
                                             vmem_limit_bytes=VMEM_LIMIT),
        name="diff_prompt",
    )(dqt, dka, dvt, dlam)


def _pad_rows(x, rows):
    return jnp.concatenate([x, jnp.zeros((rows - x.shape[0], x.shape[1]), x.dtype)], axis=0)


def _page_pipeline(pt_ref, hbm_refs, bufs, sem, n_pages):
    b, nb = pl.program_id(0), pl.num_programs(0)

    def copy(k, seq, slot, pg):
        phys = pt_ref[seq * n_pages + pg]
        return pltpu.make_async_copy(hbm_refs[k].at[phys], bufs[k].at[slot, pg], sem.at[k, slot])

    def start_all(seq, slot):
        def body(pg, carry):
            for k in range(len(bufs)):
                copy(k, seq, slot, pg).start()
            return carry
        lax.fori_loop(0, n_pages, body, 0)

    @pl.when(b == 0)
    def _():
        start_all(0, 0)

    @pl.when(b + 1 < nb)
    def _():
        start_all(b + 1, (b + 1) % 2)

    slot = b % 2

    def wait_body(pg, carry):
        for k in range(len(bufs)):
            copy(k, b, slot, pg).wait()
        return carry

    lax.fori_loop(0, n_pages, wait_body, 0)
    return slot


def _softmax_all(s_ref, p_ref):
    s3 = s_ref[...]
    m = jnp.max(jnp.max(s3, axis=0), axis=-1, keepdims=True)
    p3 = jnp.exp(s3 - m[None])
    p_ref[...] = p3.astype(BF16)
    return jnp.sum(jnp.sum(p3, axis=0), axis=-1, keepdims=True)


def _fox_sample_kernel(pt_ref, qbd_ref, lcol_ref, lrow_ref, knew_ref, vnew_ref, sup_ref, ck_hbm, cv_hbm, clf_hbm,
                       o_ref, kbuf, vbuf, lbuf, r_ref, s_ref, p_ref, sem, *, n_pages, t_new):
    slot = _page_pipeline(pt_ref, (ck_hbm, cv_hbm, clf_hbm), (kbuf, vbuf, lbuf), sem, n_pages)
    nh = FOX_HEADS
    rows = t_new * nh
    page = kbuf.shape[-1]
    qbd = qbd_ref[0]
    lc = lcol_ref[0]
    parts, run = [], None
    for t in range(t_new):
        run = lc[t * nh:(t + 1) * nh] if run is None else run + lc[t * nh:(t + 1) * nh]
        parts.append(run)
    qc = jnp.concatenate(parts, axis=0)

    lp_all = lbuf[slot].reshape(n_pages * nh, page)
    hi, mid, lo = _split3(lp_all)
    sup = sup_ref[...]
    r_in = _dot(hi, sup) + _dot(mid, sup) + _dot(lo, sup)
    tot = r_in[:, 0:1] + lp_all[:, 0:1]
    later, c = [], jnp.zeros((nh, 1), F32)
    for pg in reversed(range(n_pages)):
        later.append(c)
        c = c + tot[pg * nh:(pg + 1) * nh]
    r_ref[...] = r_in + jnp.concatenate(later[::-1], axis=0)

    qcb = jnp.broadcast_to(qc, (rows, page))

    def qk(pg, carry):
        r_pg = r_ref[pl.ds(pl.multiple_of(pg * nh, nh), nh), :]
        s_ref[pg] = _dot(qbd, kbuf[slot, pg].astype(BF16)) + jnp.concatenate([r_pg] * t_new, axis=0) + qcb
        return carry

    lax.fori_loop(0, n_pages, qk, 0, unroll=PAGE_UNROLL)

    kn = _pad_rows(knew_ref[0], page).astype(BF16)
    vn = _pad_rows(vnew_ref[0], page).astype(BF16)
    lr = lrow_ref[0]
    lane8 = lax.broadcasted_iota(jnp.int32, lr.shape, 1)
    qct = jnp.zeros(lr.shape, F32)
    for t in range(t_new):
        qct = qct + jnp.where(lane8 >= t, lr[:, t:t + 1], 0.0)
    rr = lax.broadcasted_iota(jnp.int32, (rows, page), 0)
    cc = lax.broadcasted_iota(jnp.int32, (rows, page), 1)
    s_new = _dot_nt(qbd, kn) + qc - jnp.concatenate([qct] * t_new, axis=0)
    s_ref[n_pages] = jnp.where((cc * nh <= rr) & (cc < t_new), s_new, NEG)

    l = _softmax_all(s_ref, p_ref)

    def pv(pg, acc):
        return acc + _dot_nt(p_ref[pg], vbuf[slot, pg].astype(BF16))

    acc = lax.fori_loop(0, n_pages, pv, _dot(p_ref[n_pages], vn), unroll=PAGE_UNROLL)
    o = acc / l
    o_a, o_b = o[:, :LANES], o[:, LANES:]
    kv = (lax.broadcasted_iota(jnp.int32, (rows, LANES), 0) % nh) // (FOX_HEADS // FOX_KV_HEADS)
    sel = jnp.where(kv == 0, o_a, jnp.where(kv == 1, pltpu.roll(o_a, HEAD_DIM, 1),
                    jnp.where(kv == 2, o_b, pltpu.roll(o_b, HEAD_DIM, 1))))
    o_ref[0] = sel[:, :HEAD_DIM]


def _fox_sample(pt_flat, qbd, lcol, lrow, knew, vnew, ckt, cvt, clf, n_pages):
    db, rows, _ = qbd.shape
    t_new = rows // FOX_HEADS
    page = ckt.shape[-1]
    sup = jnp.asarray(np.tril(np.ones((page, page), np.float32), -1), dtype=BF16)
    per_b = lambda shape: pl.BlockSpec((1,) + shape, lambda b, pt: (b, 0, 0))
    hbm = pl.BlockSpec(memory_space=pl.ANY)
    in_specs = [per_b((rows, MXU_DIM)), per_b((rows, 1)), per_b((FOX_HEADS, LANES)),
                per_b((8, FOX_KV_WIDTH)), per_b((8, FOX_KV_WIDTH)),
                pl.BlockSpec((page, page), lambda b, pt: (0, 0)), hbm, hbm, hbm]
    return pl.pallas_call(
        functools.partial(_fox_sample_kernel, n_pages=n_pages, t_new=t_new),
        out_shape=jax.ShapeDtypeStruct((db, rows, HEAD_DIM), F32),
        grid_spec=pltpu.PrefetchScalarGridSpec(
            num_scalar_prefetch=1,
            grid=(db,),
            in_specs=in_specs,
            out_specs=pl.BlockSpec((1, rows, HEAD_DIM), lambda b, pt: (b, 0, 0)),
            scratch_shapes=[pltpu.VMEM((2, n_pages, FOX_KV_WIDTH, page), F32),
                            pltpu.VMEM((2, n_pages, FOX_KV_WIDTH, page), F32),
                            pltpu.VMEM((2, n_pages, FOX_HEADS, page), F32),
                            pltpu.VMEM((n_pages * FOX_HEADS, page), F32),
                            pltpu.VMEM((n_pages + 1, rows, page), F32),
                            pltpu.VMEM((n_pages + 1, rows, page), BF16),
                            pltpu.SemaphoreType.DMA((3, 2))]),
        compiler_params=pltpu.CompilerParams(dimension_semantics=("arbitrary",),
                                             vmem_limit_bytes=VMEM_LIMIT),
        name="fox_sample",
    )(pt_flat, qbd, lcol, lrow, knew, vnew, sup, ckt, cvt, clf)


def _diff_sample_kernel(pt_ref, qbd_ref, knew_ref, vnew_ref, dl_ref, ck_hbm, cv_hbm,
                        o_ref, kbuf, vbuf, s_ref, p_ref, sem, *, n_pages, t_new, lam_init):
    slot = _page_pipeline(pt_ref, (ck_hbm, cv_hbm), (kbuf, vbuf), sem, n_pages)
    half = t_new * DIFF_HEADS
    rows = 2 * half
    page = kbuf.shape[-1]
    qbd = qbd_ref[0]

    def qk(pg, carry):
        s_ref[pg] = _dot(qbd, kbuf[slot, pg].astype(BF16))
        return carry

    lax.fori_loop(0, n_pages, qk, 0, unroll=PAGE_UNROLL)

    kn = _pad_rows(knew_ref[0], page).astype(BF16)
    vn = _pad_rows(vnew_ref[0], page).astype(BF16)
    rr = lax.broadcasted_iota(jnp.int32, (rows, page), 0)
    cc = lax.broadcasted_iota(jnp.int32, (rows, page), 1)
    t_row = (rr % half) // DIFF_HEADS
    s_ref[n_pages] = jnp.where((cc <= t_row) & (cc < t_new), _dot_nt(qbd, kn), NEG)

    l = _softmax_all(s_ref, p_ref)

    def pv(pg, acc):
        a0, a1 = acc
        pp = p_ref[pg]
        v0 = vbuf[slot, pg, pl.ds(0, page, stride=DIFF_KV_HEADS), :].astype(BF16)
        v1 = vbuf[slot, pg, pl.ds(1, page, stride=DIFF_KV_HEADS), :].astype(BF16)
        return a0 + _dot(pp, v0), a1 + _dot(pp, v1)

    w = 2 * HEAD_DIM
    new = _dot(p_ref[n_pages], vn)
    a0, a1 = lax.fori_loop(0, n_pages, pv, (new[:, :w], new[:, w:]), unroll=PAGE_UNROLL)
    kv = (lax.broadcasted_iota(jnp.int32, (rows, w), 0) % DIFF_HEADS) // (DIFF_HEADS // DIFF_KV_HEADS)
    sel = jnp.where(kv == 0, a0, a1) / l
    o_ref[0] = sel[:half] - _diff_lambda(dl_ref, lam_init) * sel[half:]


def _diff_sample(pt_flat, qbd, knew, vnew, dlam, ckt, cv, n_pages, lam_init):
    db, rows, _ = qbd.shape
    t_new = rows // (2 * DIFF_HEADS)
    page = ckt.shape[-1]
    per_b = lambda shape: pl.BlockSpec((1,) + shape, lambda b, pt: (b, 0, 0))
    hbm = pl.BlockSpec(memory_space=pl.ANY)
    in_specs = [per_b((rows, MXU_DIM)), per_b((8, DIFF_KV_WIDTH)), per_b((8, DIFF_KV_WIDTH)),
                pl.BlockSpec((4, HEAD_DIM), lambda b, pt: (0, 0)), hbm, hbm]
    return pl.pallas_call(
        functools.partial(_diff_sample_kernel, n_pages=n_pages, t_new=t_new, lam_init=lam_init),
        out_shape=jax.ShapeDtypeStruct((db, rows // 2, 2 * HEAD_DIM), F32),
        grid_spec=pltpu.PrefetchScalarGridSpec(
            num_scalar_prefetch=1,
            grid=(db,),
            in_specs=in_specs,
            out_specs=pl.BlockSpec((1, rows // 2, 2 * HEAD_DIM), lambda b, pt: (b, 0, 0)),
            scratch_shapes=[pltpu.VMEM((2, n_pages, DIFF_KV_WIDTH, page), F32),
                            pltpu.VMEM((2, n_pages, DIFF_KV_HEADS * page, 2 * HEAD_DIM), F32),
                            pltpu.VMEM((n_pages + 1, rows, page), F32),
                            pltpu.VMEM((n_pages + 1, rows, page), BF16),
                            pltpu.SemaphoreType.DMA((2, 2))]),
        compiler_params=pltpu.CompilerParams(dimension_semantics=("arbitrary",),
                                             vmem_limit_bytes=VMEM_LIMIT),
        name="diff_sample",
    )(pt_flat, qbd, knew, vnew, dlam, ckt, cv)


def _heads_rmsnorm_t(xt, width, gain):
    parts = []
    for h in range(xt.shape[0] // width):
        xh = xt[h * width:(h + 1) * width]
        parts.append(xh * lax.rsqrt(jnp.mean(xh * xh, axis=0, keepdims=True) + EPS))
    return jnp.concatenate(parts, axis=0) * jnp.concatenate([gain] * (xt.shape[1] // LANES), axis=1)


def _ffn_kernel(*refs, tm, d_ff, post_scale, stateful):
    if stateful:
        _ffn_tile(refs, tm, d_ff, post_scale, stateful)
        return
    carry_ref, meta_ref = refs[-2:]
    b, i = pl.program_id(0), pl.program_id(1)

    @pl.when((i > 0) | (b == 0))
    def _():
        _ffn_tile(refs[:-1], tm, d_ff, post_scale, stateful)

        @pl.when(i == 0)
        def _():
            meta_ref[...] = carry_ref[...]

    @pl.when((i == 0) & (b > 0))
    def _():
        carry_ref[...] = meta_ref[...]


def _ffn_tile(refs, tm, d_ff, post_scale, stateful):
    if stateful:
        (x_ref, fo_ref, do_ref, gm64_ref, gm128_ref, gfo_ref, gdo_ref, wo_ref, gffn_ref, wup_ref,
         cw_ref, cb_ref, wdn_ref, s0_ref, s1_ref, y_ref, h_ref, hs_ref) = refs
        fo = _group_rmsnorm(fo_ref[0], gm64_ref[...], gfo_ref[...])
        do = _group_rmsnorm(do_ref[0], gm128_ref[...], gdo_ref[...]) * post_scale
        merged = jnp.concatenate([fo, do], axis=1)
    else:
        (x_ref, head_ref, fo_ref, do_ref, gfo_ref, gdo_ref, wo_ref, gffn_ref, wup_ref,
         cw_ref, cb_ref, wdn_ref, y_ref, h_ref, hs_ref, carry_ref) = refs

        @pl.when(pl.program_id(1) == 0)
        def _():
            carry_ref[...] = jnp.zeros_like(carry_ref)

        fo = _heads_rmsnorm_t(fo_ref[0], HEAD_DIM, gfo_ref[...])
        do = _heads_rmsnorm_t(do_ref[0], 2 * HEAD_DIM, gdo_ref[...]) * post_scale
        merged = jnp.concatenate([fo, do], axis=0).T

    att = _dot(merged.astype(BF16), wo_ref[...])
    x = x_ref[0]
    if not stateful:
        x = jnp.where(pl.program_id(1) == 0, head_ref[...], x)
    x1 = x + att
    hn = x1 * lax.rsqrt(jnp.mean(x1 * x1, axis=-1, keepdims=True) + EPS) * gffn_ref[...]
    hnb = hn.astype(BF16)
    ch = FF_CHUNK
    n_chunks = d_ff // ch
    rid = lax.broadcasted_iota(jnp.int32, (8, ch), 0)

    def produce(c):
        for half in range(2):
            col = half * d_ff + c * ch
            hs_ref[c % 2, half] = _dot(hnb, wup_ref[:, col:col + ch])

    def consume(c, acc):
        conv = []
        for half in range(2):
            col = half * d_ff + c * ch
            h = hs_ref[c % 2, half]
            if stateful:
                nseq = s0_ref.shape[0]
                hm1 = jnp.concatenate([s1_ref[:, col:col + ch], h[:tm - nseq]], axis=0)
                hm2 = jnp.concatenate([s0_ref[:, col:col + ch], s1_ref[:, col:col + ch], h[:tm - 2 * nseq]], axis=0)
                h_ref[:, col:col + ch] = h[tm - 2 * nseq:]
            else:
                prev = carry_ref[:, col:col + ch]
                r1, r2 = pltpu.roll(h, 1, 0), pltpu.roll(h, 2, 0)
                top1 = jnp.where(rid < 1, pltpu.roll(prev, 1, 0), r1[0:8])
                top2 = jnp.where(rid < 2, pltpu.roll(prev, 2, 0), r2[0:8])
                hm1 = jnp.concatenate([top1, r1[8:]], axis=0)
                hm2 = jnp.concatenate([top2, r2[8:]], axis=0)
                carry_ref[:, col:col + ch] = h[tm - 8:]
                h_ref[0, :, col:col + ch] = h[tm - 8:]
            conv.append(cb_ref[:, col:col + ch] + cw_ref[0:1, col:col + ch] * hm2
                        + cw_ref[1:2, col:col + ch] * hm1 + cw_ref[2:3, col:col + ch] * h)
        g, u = conv
        act = g * (1.0 / (1.0 + jnp.exp(-g))) * u
        return acc + _dot(act.astype(BF16), wdn_ref[c * ch:(c + 1) * ch, :])

    acc = x1
    produce(0)
    for c in range(n_chunks):
        if c + 1 < n_chunks:
            produce(c + 1)
        acc = consume(c, acc)
    y_ref[0] = acc


def _merge_ffn(x, fo, do, p, tm, state=None, head=None):
    nb, lt, d = x.shape
    d_ff = p["w_down"].shape[0]
    stateful = state is not None
    single = pl.Buffered(1)
    const = lambda shape: pl.BlockSpec(shape, lambda b, i: (0,) * len(shape), pipeline_mode=single)
    weights = [const((FOX_WIDTH + DIFF_WIDTH, d)), const((1, d)), const((d, 2 * d_ff)), const((CONV_W, 2 * d_ff)),
               const((1, 2 * d_ff)), const((d_ff, d))]
    wargs = [p["w_o"], p["g_ffn"], p["w_up"], p["conv_w"], p["conv_b"], p["w_down"]]
    if stateful:
        nseq = state[0].shape[0]
        row = lambda w: pl.BlockSpec((1, tm, w), lambda b, i: (b, i, 0), pipeline_mode=single)
        in_specs = ([row(d), row(FOX_WIDTH), row(DIFF_WIDTH), const((MXU_DIM, MXU_DIM)), const((MXU_DIM, MXU_DIM)),
                     const((1, FOX_WIDTH)), const((1, DIFF_WIDTH))] + weights
                    + [const((nseq, 2 * d_ff)), const((nseq, 2 * d_ff))])
        args = [x, fo, do, p["gm64"], p["gm128"], p["g_fo"], p["g_do"]] + wargs + list(state)
        h_shape = jax.ShapeDtypeStruct((2 * nseq, 2 * d_ff), F32)
        h_spec = pl.BlockSpec((2 * nseq, 2 * d_ff), lambda b, i: (0, 0))
        scratch = [pltpu.VMEM((2, 2, tm, FF_CHUNK), F32)]
    else:
        real = pl.BlockSpec((1, tm, d), lambda b, i: (b, jnp.maximum(i - 1, 0), 0))
        in_specs = ([real, const((tm, d)),
                     pl.BlockSpec((1, FOX_WIDTH, tm), lambda b, i: (b, 0, i)),
                     pl.BlockSpec((1, DIFF_WIDTH, tm), lambda b, i: (b, 0, i)),
                     const((FOX_WIDTH, LANES)), const((DIFF_WIDTH, LANES))] + weights)
        args = [x, head, fo, do, p["g_fo_t"], p["g_do_t"]] + wargs
        h_shape = jax.ShapeDtypeStruct((nb, 8, 2 * d_ff), F32)
        h_spec = pl.BlockSpec((1, 8, 2 * d_ff), lambda b, i: (b, 0, 0))
        scratch = [pltpu.VMEM((2, 2, tm, FF_CHUNK), F32), pltpu.VMEM((8, 2 * d_ff), F32),
                   pltpu.VMEM((8, 2 * d_ff), F32)]
    y_spec = pl.BlockSpec((1, tm, d), lambda b, i: (b, i, 0)) if stateful else real
    n_tiles = lt // tm + (0 if stateful else 1)
    kern = functools.partial(_ffn_kernel, tm=tm, d_ff=d_ff, post_scale=p["post_scale"], stateful=stateful)
    return pl.pallas_call(
        kern,
        out_shape=[jax.ShapeDtypeStruct((nb, lt, d), F32), h_shape],
        grid=(nb, n_tiles),
        in_specs=in_specs,
        out_specs=[y_spec, h_spec],
        scratch_shapes=scratch,
        compiler_params=pltpu.CompilerParams(dimension_semantics=("arbitrary", "arbitrary"),
                                             vmem_limit_bytes=VMEM_LIMIT),
        name="merge_ffn_sample" if stateful else "merge_ffn_prompt",
    )(*args)


def kernel(x_prompt, x_sample, cache_fox_k, cache_fox_v, cache_fox_logf, cache_diff_k, cache_diff_v, state_ffn_conv, page_table, meta_tokens, attn_norm_g, w_in, b_f, fox_qn_g, fox_kn_g, fox_on_g, diff_qn_g, diff_kn_g, diff_lambda, diff_subln_g, w_o, ffn_norm_g, w_up, conv_w, conv_b, w_down):
    nb, seq, d = x_prompt.shape
    db, t_new, _ = x_sample.shape
    depth, n_phys, page = cache_fox_k.shape[:3]
    n_pages = page_table.shape[1]
    assert depth == 1, "single layer only"
    layer = 0
    lam_init = 0.8 - 0.6 * math.exp(-0.3 * layer)
    past_len = n_pages * page

    wi = w_in[layer]
    o = np.cumsum([0, FOX_WIDTH, FOX_KV_WIDTH, FOX_KV_WIDTH, FOX_HEADS, DIFF_WIDTH, DIFF_KV_WIDTH, DIFF_KV_WIDTH])
    seg = lambda k: wi[:, o[k]:o[k + 1]]
    w_all = jnp.concatenate([seg(0), seg(1), seg(2), seg(4), seg(5), seg(6),
                             jnp.pad(seg(3), ((0, 0), (0, LANES - FOX_HEADS)))], axis=1).astype(BF16)
    g_fo = fox_on_g[layer].reshape(1, FOX_WIDTH)
    g_do = jnp.tile(diff_subln_g[layer], DIFF_HEADS)[None, :]
    lane_rep = lambda g: jnp.broadcast_to(g.reshape(-1, 1), (g.shape[-1], LANES))
    p = {
        "g_attn": attn_norm_g[layer][None, :],
        "w_in": w_all,
        "gm64": _block_diag_mean(HEAD_DIM),
        "gm128": _block_diag_mean(2 * HEAD_DIM),
        "g_heads": jnp.concatenate([jnp.tile(fox_qn_g[layer], FOX_HEADS), jnp.tile(fox_kn_g[layer], FOX_KV_HEADS),
                                    jnp.tile(diff_qn_g[layer], 2 * DIFF_HEADS),
                                    jnp.tile(diff_kn_g[layer], 2 * DIFF_KV_HEADS)])[None, :],
        "b_f": jnp.pad(b_f[layer], (0, LANES - FOX_HEADS))[None, :],
        "g_fo": g_fo, "g_do": g_do, "g_fo_t": lane_rep(g_fo), "g_do_t": lane_rep(g_do),
        "post_scale": 1.0 - lam_init,
        "w_o": w_o[layer].astype(BF16),
        "g_ffn": ffn_norm_g[layer][None, :],
        "w_up": w_up[layer].astype(BF16),
        "conv_w": conv_w[layer],
        "conv_b": conv_b[layer][None, :],
        "w_down": w_down[layer].astype(BF16),
    }
    dlam = diff_lambda[layer]

    tm = ROW_TILE
    pad = (-N_META) % tm
    lp = pad + N_META + seq
    assert pad + N_META == tm and seq % tm == 0
    head = jnp.concatenate([jnp.zeros((pad, d), x_prompt.dtype), meta_tokens.astype(x_prompt.dtype)], axis=0)
    pos_p = jnp.arange(lp) - pad
    fk, fv, lf, dk, dv, qat, ka, vt, dqt, dka, dvt = _project(x_prompt, pos_p, tm, p, head=head)
    fo_t = _fox_prompt(qat, ka, vt, tm, pad)
    do_t = _diff_prompt(dqt, dka, dvt, dlam, tm, pad, lam_init)
    y_prompt, hlast = _merge_ffn(x_prompt, fo_t, do_t, p, tm, head=head)
    ltot = N_META + seq
    pk = fk[:, pad:].reshape(1, nb, ltot, FOX_KV_HEADS, HEAD_DIM)
    pv = fv[:, pad:].reshape(1, nb, ltot, FOX_KV_HEADS, HEAD_DIM)
    pf = lf[:, pad:][None]
    pdk = dk[:, pad:].reshape(1, nb, ltot, DIFF_KV_HEADS, 2, HEAD_DIM)
    pdv = dv[:, pad:].reshape(1, nb, ltot, DIFF_KV_HEADS, 2 * HEAD_DIM)
    pc = hlast[:, 8 - (CONV_W - 1):][None]

    rows = db * t_new
    pos_s = past_len + (jnp.arange(rows) % t_new)
    qf, fk, fv, lf, dqb, dk, dv = _project(x_sample.reshape(1, rows, d), pos_s, min(ROW_TILE, rows), p)
    g_f = FOX_HEADS // FOX_KV_HEADS
    g_d = DIFF_HEADS // DIFF_KV_HEADS
    eye_f = jnp.eye(FOX_KV_HEADS, dtype=BF16)
    qbd_f = jnp.einsum('btkgd,kj->btkgjd', qf.reshape(db, t_new, FOX_KV_HEADS, g_f, HEAD_DIM), eye_f)
    qbd_f = qbd_f.reshape(db, t_new * FOX_HEADS, FOX_KV_WIDTH)
    eye_d, eye_2 = jnp.eye(DIFF_KV_HEADS, dtype=BF16), jnp.eye(2, dtype=BF16)
    qbd_d = jnp.einsum('btkgwd,kj,wv->bwtkgjvd', dqb.reshape(db, t_new, DIFF_KV_HEADS, g_d, 2, HEAD_DIM), eye_d, eye_2)
    qbd_d = qbd_d.reshape(db, 2 * t_new * DIFF_HEADS, DIFF_KV_WIDTH)
    lf_s = lf.reshape(db, t_new, FOX_HEADS)
    lcol = lf_s.reshape(db, t_new * FOX_HEADS, 1)
    lrow = jnp.pad(jnp.swapaxes(lf_s, 1, 2), ((0, 0), (0, 0), (0, LANES - t_new)))
    new_rows = lambda a: jnp.pad(a.reshape(db, t_new, a.shape[-1]), ((0, 0), (0, 8 - t_new), (0, 0)))
    pt_flat = page_table.reshape(-1).astype(jnp.int32)
    ckt = jnp.transpose(cache_fox_k[layer], (0, 2, 3, 1)).reshape(n_phys, FOX_KV_WIDTH, page)
    cvt = jnp.transpose(cache_fox_v[layer], (0, 2, 3, 1)).reshape(n_phys, FOX_KV_WIDTH, page)
    clf = jnp.swapaxes(cache_fox_logf[layer], 1, 2)
    fo_s = _fox_sample(pt_flat, qbd_f, lcol, lrow, new_rows(fk), new_rows(fv), ckt, cvt, clf, n_pages)
    cdkt = jnp.transpose(cache_diff_k[layer], (0, 2, 3, 4, 1)).reshape(n_phys, DIFF_KV_WIDTH, page)
    cdv = cache_diff_v[layer].reshape(n_phys, page * DIFF_KV_HEADS, 2 * HEAD_DIM)
    do_s = _diff_sample(pt_flat, qbd_d, new_rows(dk), new_rows(dv), dlam, cdkt, cdv, n_pages, lam_init)

    tmajor = lambda a, w: jnp.swapaxes(a.reshape(db, t_new, w), 0, 1).reshape(1, rows, w)
    st = state_ffn_conv[layer]
    ys, hs = _merge_ffn(tmajor(x_sample, d), tmajor(fo_s, FOX_WIDTH), tmajor(do_s, DIFF_WIDTH), p, rows,
                        state=(st[:, 0], st[:, 1]))
    y_sample = jnp.swapaxes(ys.reshape(t_new, db, d), 0, 1)
    sc = jnp.swapaxes(hs.reshape(CONV_W - 1, db, hs.shape[-1]), 0, 1)[None]
    sk = fk.reshape(1, db, t_new, FOX_KV_HEADS, HEAD_DIM)
    sv = fv.reshape(1, db, t_new, FOX_KV_HEADS, HEAD_DIM)
    sf = lf_s[None]
    sdk = dk.reshape(1, db, t_new, DIFF_KV_HEADS, 2, HEAD_DIM)
    sdv = dv.reshape(1, db, t_new, DIFF_KV_HEADS, 2 * HEAD_DIM)
    return (y_prompt, y_sample, pk, pv, pf, pdk, pdv, pc, sk, sv, sf, sdk, sdv, sc)
```

```python
import functools
import math

import numpy as np
import jax
import jax.numpy as jnp
from jax import lax
from jax.experimental import pallas as pl
from jax.experimental.pallas import tpu as pltpu

F32 = jnp.float32
BF16 = jnp.bfloat16

HEAD_DIM = 64
FOX_HEADS = 8
FOX_KV_HEADS = 4
DIFF_HEADS = 4
DIFF_KV_HEADS = 2
FOX_WIDTH = FOX_HEADS * HEAD_DIM
FOX_KV_WIDTH = FOX_KV_HEADS * HEAD_DIM
DIFF_WIDTH = DIFF_HEADS * 2 * HEAD_DIM
DIFF_KV_WIDTH = DIFF_KV_HEADS * 2 * HEAD_DIM
ROT_DIM = HEAD_DIM // 4
ROPE_THETA = 500000.0
CONV_W = 3
N_META = 16
EPS = 1e-6
NEG = -1e30
SCALE = HEAD_DIM ** -0.5
LOG2E = 1.4426950408889634

LANES = 128
MXU_DIM = 256
ROW_TILE = 256
FF_CHUNK = 256
PAGE_UNROLL = 16
VMEM_LIMIT = 56 * 1024 * 1024
VT_PAD = 16
FOX_VT_ROWS = FOX_KV_HEADS * (HEAD_DIM + VT_PAD)
DIFF_VT_ROWS = DIFF_KV_HEADS * (2 * HEAD_DIM + VT_PAD)

_FQ, _FK, _FV, _DQ, _DK, _DV, _FF, _W_TOTAL = 0, 512, 768, 1024, 1536, 1792, 2048, 2176
_AUG_C, _AUG_G0 = HEAD_DIM, HEAD_DIM + 3


def _split3(x):
    hi = x.astype(BF16)
    r = x - hi.astype(F32)
    mid = r.astype(BF16)
    lo = (r - mid.astype(F32)).astype(BF16)
    return hi, mid, lo


def _dot(a, b):
    return jnp.dot(a, b, preferred_element_type=F32)


def _dot_nt(a, b):
    return lax.dot_general(a, b, (((1,), (1,)), ((), ())), preferred_element_type=F32)


def _group_mean_sq(x, gmat):
    outs = []
    for c in range(x.shape[1] // MXU_DIM):
        s = x[:, c * MXU_DIM:(c + 1) * MXU_DIM]
        outs.append(_dot((s * s).astype(BF16), gmat))
    return outs[0] if len(outs) == 1 else jnp.concatenate(outs, axis=1)


def _group_rmsnorm(x, gmat, gain):
    return x * lax.rsqrt(_group_mean_sq(x, gmat) + EPS) * gain


def _proj_kernel(*refs, prompt):
    (x_ref, gattn_ref, w_ref, gm_ref, gh_ref, bf_ref, cos_ref, sa_ref, sb_ref) = refs[:9]
    if prompt:
        (tri_ref, pq_ref, pk_ref, oq_ref, ok_ref, head_ref,
         fk_ref, fv_ref, lf_ref, dk_ref, dv_ref,
         qat_ref, ka_ref, vt_ref, dqt_ref, dka_ref, dvt_ref, carry_ref) = refs[9:]
    else:
        (qf_ref, fk_ref, fv_ref, lf_ref, dqb_ref, dk_ref, dv_ref) = refs[9:]
    x = x_ref[0]
    if prompt:
        x = jnp.where(pl.program_id(1) == 0, head_ref[...], x)
    xn = x * lax.rsqrt(jnp.mean(x * x, axis=-1, keepdims=True) + EPS) * gattn_ref[...]
    y = _dot(xn.astype(BF16), w_ref[...])
    gm = gm_ref[...]
    cos_t, sin_a, sin_b = cos_ref[...], sa_ref[...], sb_ref[...]

    def rope(seg):
        outs = []
        for c in range(seg.shape[1] // LANES):
            s = seg[:, c * LANES:(c + 1) * LANES]
            outs.append(s * cos_t + pltpu.roll(s, LANES - ROT_DIM // 2, 1) * sin_a
                        + pltpu.roll(s, ROT_DIM // 2, 1) * sin_b)
        return jnp.concatenate(outs, axis=1)

    fq = _group_rmsnorm(y[:, _FQ:_FK], gm, gh_ref[:, 0:512])
    fk = _group_rmsnorm(y[:, _FK:_FV], gm, gh_ref[:, 512:768])
    fv = y[:, _FV:_DQ]
    dq = rope(_group_rmsnorm(y[:, _DQ:_DK], gm, gh_ref[:, 768:1280]))
    dk = rope(_group_rmsnorm(y[:, _DK:_DV], gm, gh_ref[:, 1280:1536]))
    dv = y[:, _DV:_FF]
    z = y[:, _FF:_W_TOTAL] + bf_ref[...]
    lf = jnp.minimum(z, 0.0) - jnp.log1p(jnp.exp(-jnp.abs(z)))
    fk_ref[0] = fk
    fv_ref[0] = fv
    dk_ref[0] = dk
    dv_ref[0] = dv
    lf_ref[0] = lf[:, :FOX_HEADS]
    if not prompt:
        qf_ref[0] = (fq * SCALE).astype(BF16)
        dqb_ref[0] = (dq * SCALE).astype(BF16)
        return

    lane = lax.broadcasted_iota(jnp.int32, lf.shape, 1)
    lf = jnp.where(lane < FOX_HEADS, lf, 0.0)
    hi, mid, lo = _split3(lf)
    tri = tri_ref[...]
    cs = _dot(tri, hi) + _dot(tri, mid) + _dot(tri, lo)

    @pl.when(pl.program_id(1) == 0)
    def _():
        carry_ref[...] = jnp.zeros_like(carry_ref)

    c = cs + carry_ref[0:1, :]
    tm = c.shape[0]
    carry_ref[0:1, :] = c[tm - 1:tm, :]

    x3 = jnp.concatenate(_split3(c * LOG2E), axis=1)
    eq = _dot(x3, pq_ref[...]) + oq_ref[...]
    ek = _dot(x3, pk_ref[...]) + ok_ref[...]
    low = lane < HEAD_DIM
    half = lambda chunk, odd: pltpu.roll(chunk, HEAD_DIM, 1) if odd else chunk
    fqs = fq * (SCALE * LOG2E)
    for h in range(FOX_HEADS):
        chunk = fqs[:, (h // 2) * LANES:(h // 2 + 1) * LANES]
        qa = jnp.where(low, half(chunk, h % 2), eq[:, h * LANES:(h + 1) * LANES])
        qat_ref[0, h * LANES:(h + 1) * LANES, :] = qa.T.astype(BF16)
    for h in range(FOX_KV_HEADS):
        chunk = fk[:, (h // 2) * LANES:(h // 2 + 1) * LANES]
        ka = jnp.where(low, half(chunk, h % 2), ek[:, h * LANES:(h + 1) * LANES])
        ka_ref[0, :, h * LANES:(h + 1) * LANES] = ka.astype(BF16)
    ones_rows = jnp.where(lax.broadcasted_iota(jnp.int32, (VT_PAD, tm), 0) == 0, 1.0, 0.0).astype(BF16)
    fw = HEAD_DIM + VT_PAD
    for c2 in range(FOX_KV_WIDTH // LANES):
        t = fv[:, c2 * LANES:(c2 + 1) * LANES].T.astype(BF16)
        for k in range(LANES // HEAD_DIM):
            h = c2 * (LANES // HEAD_DIM) + k
            vt_ref[0, h * fw:h * fw + HEAD_DIM, :] = t[k * HEAD_DIM:(k + 1) * HEAD_DIM]
            vt_ref[0, h * fw + HEAD_DIM:(h + 1) * fw, :] = ones_rows
    dqs = dq * (SCALE * LOG2E)
    for c2 in range(DIFF_WIDTH // LANES):
        dqt_ref[0, c2 * LANES:(c2 + 1) * LANES, :] = dqs[:, c2 * LANES:(c2 + 1) * LANES].T.astype(BF16)
    for h in range(2 * DIFF_KV_HEADS):
        chunk = dk[:, (h // 2) * LANES:(h // 2 + 1) * LANES]
        dka_ref[0, :, h * LANES:(h + 1) * LANES] = jnp.where(low, half(chunk, h % 2), 0.0).astype(BF16)
    dw = 2 * HEAD_DIM + VT_PAD
    for h in range(DIFF_KV_HEADS):
        dvt_ref[0, h * dw:h * dw + 2 * HEAD_DIM, :] = dv[:, h * LANES:(h + 1) * LANES].T.astype(BF16)
        dvt_ref[0, h * dw + 2 * HEAD_DIM:(h + 1) * dw, :] = ones_rows


def _rope_tables(pos):
    inv_freq = ROPE_THETA ** (-jnp.arange(0, ROT_DIM, 2, dtype=F32) / ROT_DIM)
    ang = pos.astype(F32)[:, None] * inv_freq[None, :]
    cos, sin = jnp.cos(ang), jnp.sin(ang)
    half = ROT_DIM // 2
    ones = jnp.ones((pos.shape[0], HEAD_DIM - ROT_DIM), F32)
    zeros_h = jnp.zeros((pos.shape[0], half), F32)
    zeros_r = jnp.zeros((pos.shape[0], HEAD_DIM - ROT_DIM), F32)
    cos_h = jnp.concatenate([cos, cos, ones], axis=1)
    sa_h = jnp.concatenate([-sin, zeros_h, zeros_r], axis=1)
    sb_h = jnp.concatenate([zeros_h, sin, zeros_r], axis=1)
    rep = LANES // HEAD_DIM
    return jnp.tile(cos_h, (1, rep)), jnp.tile(sa_h, (1, rep)), jnp.tile(sb_h, (1, rep))


def _block_diag_mean(group):
    idx = np.arange(MXU_DIM) // group
    return jnp.asarray((idx[:, None] == idx[None, :]).astype(np.float32) / group, dtype=BF16)


def _bias_placement():
    g = FOX_HEADS // FOX_KV_HEADS
    pq = np.zeros((3 * LANES, FOX_HEADS * LANES), np.float32)
    pk = np.zeros((3 * LANES, FOX_KV_HEADS * LANES), np.float32)
    oq = np.zeros((1, FOX_HEADS * LANES), np.float32)
    ok = np.zeros((1, FOX_KV_HEADS * LANES), np.float32)
    for h in range(FOX_HEADS):
        kvh, gi = divmod(h, g)
        for piece in range(3):
            pq[piece * LANES + h, h * LANES + _AUG_C + piece] = 1.0
            pk[piece * LANES + h, kvh * LANES + _AUG_G0 + 3 * gi + piece] = -1.0
            oq[0, h * LANES + _AUG_G0 + 3 * gi + piece] = 1.0
            ok[0, kvh * LANES + _AUG_C + piece] = 1.0
    return jnp.asarray(pq, BF16), jnp.asarray(pk, BF16), jnp.asarray(oq), jnp.asarray(ok)


def _project(x, pos, tm, p, head=None):
    prompt = head is not None
    nb, lt, d = x.shape
    lt += tm if prompt else 0
    nt = lt // tm
    cos_t, sin_a, sin_b = _rope_tables(pos)
    row = lambda w, dt=None: pl.BlockSpec((1, tm, w), lambda b, i: (b, i, 0))
    col = lambda w: pl.BlockSpec((1, w, tm), lambda b, i: (b, 0, i))
    const = lambda shape: pl.BlockSpec(shape, lambda b, i: (0,) * len(shape))
    tab = pl.BlockSpec((tm, LANES), lambda b, i: (i, 0))
    sds = jax.ShapeDtypeStruct
    x_spec = pl.BlockSpec((1, tm, d), lambda b, i: (b, jnp.maximum(i - 1, 0), 0)) if prompt else row(d)
    in_specs = [x_spec, const((1, d)), const((d, _W_TOTAL)), const((MXU_DIM, MXU_DIM)),
                const((1, 1536)), const((1, LANES)), tab, tab, tab]
    args = [x, p["g_attn"], p["w_in"], p["gm64"], p["g_heads"], p["b_f"], cos_t, sin_a, sin_b]
    f32_rows = [sds((nb, lt, FOX_KV_WIDTH), F32), sds((nb, lt, FOX_KV_WIDTH), F32), sds((nb, lt, FOX_HEADS), F32),
                sds((nb, lt, DIFF_KV_WIDTH), F32), sds((nb, lt, DIFF_KV_WIDTH), F32)]
    f32_specs = [row(FOX_KV_WIDTH), row(FOX_KV_WIDTH), row(FOX_HEADS), row(DIFF_KV_WIDTH), row(DIFF_KV_WIDTH)]
    if prompt:
        tri = jnp.asarray(np.tril(np.ones((tm, tm), np.float32)), dtype=BF16)
        pq, pk, oq, ok = _bias_placement()
        in_specs += [const((tm, tm)), const(pq.shape), const(pk.shape), const(oq.shape), const(ok.shape),
                     const((tm, d))]
        args += [tri, pq, pk, oq, ok, head]
        out_shape = f32_rows + [sds((nb, FOX_HEADS * LANES, lt), BF16), sds((nb, lt, FOX_KV_HEADS * LANES), BF16),
                                sds((nb, FOX_VT_ROWS, lt), BF16), sds((nb, DIFF_WIDTH, lt), BF16),
                                sds((nb, lt, 2 * DIFF_KV_HEADS * LANES), BF16), sds((nb, DIFF_VT_ROWS, lt), BF16)]
        out_specs = f32_specs + [col(FOX_HEADS * LANES), row(FOX_KV_HEADS * LANES), col(FOX_VT_ROWS),
                                 col(DIFF_WIDTH), row(2 * DIFF_KV_HEADS * LANES), col(DIFF_VT_ROWS)]
        scratch = [pltpu.VMEM((8, LANES), F32)]
    else:
        out_shape = ([sds((nb, lt, FOX_WIDTH), BF16)] + f32_rows[:3] + [sds((nb, lt, DIFF_WIDTH), BF16)] + f32_rows[3:])
        out_specs = [row(FOX_WIDTH)] + f32_specs[:3] + [row(DIFF_WIDTH)] + f32_specs[3:]
        scratch = []
    return pl.pallas_call(
        functools.partial(_proj_kernel, prompt=prompt),
        out_shape=out_shape,
        grid=(nb, nt),
        in_specs=in_specs,
        out_specs=out_specs,
        scratch_shapes=scratch,
        compiler_params=pltpu.CompilerParams(dimension_semantics=("parallel", "arbitrary"),
                                             vmem_limit_bytes=VMEM_LIMIT),
        name="proj_prompt" if prompt else "proj_sample",
    )(*args)


def _reset(m_ref, acc_ref):
    m_ref[...] = jnp.full(m_ref.shape, NEG, F32)
    acc_ref[...] = jnp.zeros(acc_ref.shape, F32)


def _flash_block_t(st, vt, m_ref, acc_ref):
    m_prev = m_ref[...]
    m_new = jnp.maximum(m_prev, jnp.max(st, axis=0, keepdims=True))
    alpha = jnp.exp2(m_prev - m_new)
    p = jnp.exp2(st - m_new)
    acc_ref[...] = alpha * acc_ref[...] + _dot(vt, p.astype(BF16))
    m_ref[...] = m_new


def _normalised(acc, dims):
    return acc[:dims] / acc[dims:dims + 1]


def _causal_sweep(i, produce, consume):
    produce(0, 0)

    @pl.when(i == 0)
    def _():
        consume(0, 0, "both")

    @pl.when(i > 0)
    def _():
        produce(1, 1)
        consume(0, 0, "pad")

        def body(jj, carry):
            j = 2 * jj + 1
            produce(j + 1, 0)
            consume(j, 1, False)

            @pl.when(j + 1 < i)
            def _():
                produce(j + 2, 1)
                consume(j + 1, 0, False)
            return carry

        lax.fori_loop(0, i // 2, body, 0)

        @pl.when(i % 2 == 1)
        def _():
            consume(i, 1, "causal")

        @pl.when(i % 2 == 0)
        def _():
            consume(i, 0, "causal")


def _key_mask(kind, tq, nq, pad):
    krow = lax.broadcasted_iota(jnp.int32, (tq, 1), 0)
    if kind == "pad":
        return krow >= pad
    lane = lax.broadcasted_iota(jnp.int32, (1, nq), 1)
    causal = krow <= jnp.where(lane >= tq, lane - tq, lane)
    return causal if kind == "causal" else causal & (krow >= pad)


def _once_for_head_tile(o_ref, tile_fn):
    b, i = pl.program_id(0), pl.program_id(1)

    @pl.when((i > 0) | (b == 0))
    def _():
        tile_fn()

    @pl.when((i == 0) & (b > 0))
    def _():
        o_ref[...] = jnp.zeros(o_ref.shape, o_ref.dtype)


def _fox_prompt_kernel(qt_ref, ka_ref, vt_ref, o_ref, *scratch, tq, pad):
    _once_for_head_tile(o_ref, lambda: _fox_prompt_tile(qt_ref, ka_ref, vt_ref, o_ref, *scratch, tq=tq, pad=pad))


def _fox_prompt_tile(qt_ref, ka_ref, vt_ref, o_ref, m_ref, acc_ref, s_ref, *, tq, pad):
    i = pl.program_id(1)
    hd = HEAD_DIM
    vw = HEAD_DIM + VT_PAD
    _reset(m_ref, acc_ref)

    def produce(j, slot):
        ks = pl.multiple_of(j * tq, tq)
        for kvh in range(FOX_KV_HEADS):
            h0, h1 = 2 * kvh, 2 * kvh + 1
            qt = jnp.concatenate([qt_ref[0, h0 * LANES:(h0 + 1) * LANES, :],
                                  qt_ref[0, h1 * LANES:(h1 + 1) * LANES, :]], axis=1)
            s_ref[slot, kvh] = _dot(ka_ref[0, pl.ds(ks, tq), kvh * LANES:(kvh + 1) * LANES], qt)

    def consume(j, slot, masked):
        ks = pl.multiple_of(j * tq, tq)
        mask = _key_mask(masked, tq, 2 * tq, pad) if masked else None
        for kvh in range(FOX_KV_HEADS):
            st = s_ref[slot, kvh]
            if masked:
                st = jnp.where(mask, st, NEG)
            _flash_block_t(st, vt_ref[0, kvh * vw:(kvh + 1) * vw, pl.ds(ks, tq)], m_ref.at[kvh], acc_ref.at[kvh])

    _causal_sweep(i, produce, consume)
    for kvh in range(FOX_KV_HEADS):
        h0, h1 = 2 * kvh, 2 * kvh + 1
        o = _normalised(acc_ref[kvh], hd)
        o_ref[0, h0 * hd:(h0 + 1) * hd, :] = o[:, :tq]
        o_ref[0, h1 * hd:(h1 + 1) * hd, :] = o[:, tq:]


def _fox_prompt(qat, ka, vt, tq, pad):
    nb, _, lp = qat.shape
    return pl.pallas_call(
        functools.partial(_fox_prompt_kernel, tq=tq, pad=pad),
        out_shape=jax.ShapeDtypeStruct((nb, FOX_WIDTH, lp), F32),
        grid=(nb, lp // tq),
        in_specs=[pl.BlockSpec((1, FOX_HEADS * LANES, tq), lambda b, i: (b, 0, i)),
                  pl.BlockSpec((1, lp, FOX_KV_HEADS * LANES), lambda b, i: (b, 0, 0)),
                  pl.BlockSpec((1, FOX_VT_ROWS, lp), lambda b, i: (b, 0, 0))],
        out_specs=pl.BlockSpec((1, FOX_WIDTH, tq), lambda b, i: (b, 0, i)),
        scratch_shapes=[pltpu.VMEM((FOX_KV_HEADS, 1, 2 * tq), F32),
                        pltpu.VMEM((FOX_KV_HEADS, HEAD_DIM + VT_PAD, 2 * tq), F32),
                        pltpu.VMEM((2, FOX_KV_HEADS, tq, 2 * tq), F32)],
        compiler_params=pltpu.CompilerParams(dimension_semantics=("parallel", "arbitrary"),
                                             vmem_limit_bytes=VMEM_LIMIT),
        name="fox_prompt",
    )(qat, ka, vt)


def _diff_lambda(dl_ref, lam_init):
    dl = dl_ref[...]
    a = jnp.sum(dl[0:1] * dl[1:2], axis=-1, keepdims=True)
    b = jnp.sum(dl[2:3] * dl[3:4], axis=-1, keepdims=True)
    return jnp.exp(a) - jnp.exp(b) + lam_init


def _diff_prompt_kernel(qt_ref, ka_ref, vt_ref, dl_ref, o_ref, *scratch, tq, pad, lam_init):
    _once_for_head_tile(o_ref, lambda: _diff_prompt_tile(qt_ref, ka_ref, vt_ref, dl_ref, o_ref, *scratch,
                                                          tq=tq, pad=pad, lam_init=lam_init))


def _diff_prompt_tile(qt_ref, ka_ref, vt_ref, dl_ref, o_ref, m_ref, acc_ref, s_ref, *, tq, pad, lam_init):
    i = pl.program_id(1)
    hd = HEAD_DIM
    vw = 2 * HEAD_DIM + VT_PAD
    _reset(m_ref, acc_ref)
    streams = [(kvh, w) for kvh in range(DIFF_KV_HEADS) for w in range(2)]

    def produce(j, slot):
        ks = pl.multiple_of(j * tq, tq)
        for n, (kvh, w) in enumerate(streams):
            r0 = (kvh * 4 + w) * hd
            qt = jnp.concatenate([qt_ref[0, r0:r0 + hd, :], qt_ref[0, r0 + 2 * hd:r0 + 3 * hd, :]], axis=1)
            s_ref[slot, n] = _dot(ka_ref[0, pl.ds(ks, tq), n * LANES:n * LANES + hd], qt)

    def consume(j, slot, masked):
        ks = pl.multiple_of(j * tq, tq)
        mask = _key_mask(masked, tq, 2 * tq, pad) if masked else None
        for n, (kvh, w) in enumerate(streams):
            st = s_ref[slot, n]
            if masked:
                st = jnp.where(mask, st, NEG)
            _flash_block_t(st, vt_ref[0, kvh * vw:(kvh + 1) * vw, pl.ds(ks, tq)], m_ref.at[n], acc_ref.at[n])

    _causal_sweep(i, produce, consume)
    lam = _diff_lambda(dl_ref, lam_init)
    w = 2 * hd
    for kvh in range(DIFF_KV_HEADS):
        n1, n2 = 2 * kvh, 2 * kvh + 1
        o = _normalised(acc_ref[n1], w) - lam * _normalised(acc_ref[n2], w)
        o_ref[0, (2 * kvh) * w:(2 * kvh + 1) * w, :] = o[:, :tq]
        o_ref[0, (2 * kvh + 1) * w:(2 * kvh + 2) * w, :] = o[:, tq:]


def _diff_prompt(dqt, dka, dvt, dlam, tq, pad, lam_init):
    nb, _, lp = dqt.shape
    ns = 2 * DIFF_KV_HEADS
    return pl.pallas_call(
        functools.partial(_diff_prompt_kernel, tq=tq, pad=pad, lam_init=lam_init),
        out_shape=jax.ShapeDtypeStruct((nb, DIFF_WIDTH, lp), F32),
        grid=(nb, lp // tq),
        in_specs=[pl.BlockSpec((1, DIFF_WIDTH, tq), lambda b, i: (b, 0, i)),
                  pl.BlockSpec((1, lp, 2 * DIFF_KV_HEADS * LANES), lambda b, i: (b, 0, 0)),
                  pl.BlockSpec((1, DIFF_VT_ROWS, lp), lambda b, i: (b, 0, 0)),
                  pl.BlockSpec((4, HEAD_DIM), lambda b, i: (0, 0))],
        out_specs=pl.BlockSpec((1, DIFF_WIDTH, tq), lambda b, i: (b, 0, i)),
        scratch_shapes=[pltpu.VMEM((ns, 1, 2 * tq), F32),
                        pltpu.VMEM((ns, 2 * HEAD_DIM + VT_PAD, 2 * tq), F32),
                        pltpu.VMEM((2, ns, tq, 2 * tq), F32)],
        compiler_params=pltpu.CompilerParams(dimension_semantics=("parallel", "arbitrary"),
                                             vmem_limit_bytes=VMEM_LIMIT),
        name="diff_prompt",
    )(dqt, dka, dvt, dlam)


def _pad_rows(x, rows):
    return jnp.concatenate([x, jnp.zeros((rows - x.shape[0], x.shape[1]), x.dtype)], axis=0)


def _page_pipeline(pt_ref, hbm_refs, bufs, sem, n_pages):
    b, nb = pl.program_id(0), pl.num_programs(0)

    def copies(seq, slot, pg):
        phys = pt_ref[seq * n_pages + pg]
        return [pltpu.make_async_copy(h.at[phys], buf.at[slot, pg], sem.at[k, slot])
                for k, (h, buf) in enumerate(zip(hbm_refs, bufs))]

    def for_pages(seq, slot, start):
        def body(pg, carry):
            for cp in copies(seq, slot, pg):
                cp.start() if start else cp.wait()
            return carry
        lax.fori_loop(0, n_pages, body, 0)

    @pl.when(b == 0)
    def _():
        for_pages(0, 0, True)

    @pl.when(b + 1 < nb)
    def _():
        for_pages(b + 1, (b + 1) % 2, True)

    slot = b % 2
    for_pages(b, slot, False)
    return slot


def _softmax_all(s_ref, p_ref):
    s3 = s_ref[...]
    m = jnp.max(jnp.max(s3, axis=0), axis=-1, keepdims=True)
    p3 = jnp.exp(s3 - m[None])
    p_ref[...] = p3.astype(BF16)
    return jnp.sum(jnp.sum(p3, axis=0), axis=-1, keepdims=True)


def _fox_sample_kernel(pt_ref, qbd_ref, lcol_ref, lrow_ref, knew_ref, vnew_ref, sup_ref, ck_hbm, cv_hbm, clf_hbm,
                       o_ref, kbuf, vbuf, lbuf, r_ref, s_ref, p_ref, sem, *, n_pages, t_new):
    slot = _page_pipeline(pt_ref, (ck_hbm, cv_hbm, clf_hbm), (kbuf, vbuf, lbuf), sem, n_pages)
    nh = FOX_HEADS
    rows = t_new * nh
    page = kbuf.shape[-1]
    qbd = qbd_ref[0]
    lc = lcol_ref[0]
    parts, run = [], None
    for t in range(t_new):
        run = lc[t * nh:(t + 1) * nh] if run is None else run + lc[t * nh:(t + 1) * nh]
        parts.append(run)
    qc = jnp.concatenate(parts, axis=0)

    lp_all = lbuf[slot].reshape(n_pages * nh, page)
    hi, mid, lo = _split3(lp_all)
    sup = sup_ref[...]
    r_in = _dot(hi, sup) + _dot(mid, sup) + _dot(lo, sup)
    tot = r_in[:, 0:1] + lp_all[:, 0:1]
    later, c = [], jnp.zeros((nh, 1), F32)
    for pg in reversed(range(n_pages)):
        later.append(c)
        c = c + tot[pg * nh:(pg + 1) * nh]
    r_ref[...] = r_in + jnp.concatenate(later[::-1], axis=0)

    qcb = jnp.broadcast_to(qc, (rows, page))

    def qk(pg, carry):
        r_pg = r_ref[pl.ds(pl.multiple_of(pg * nh, nh), nh), :]
        s_ref[pg] = _dot(qbd, kbuf[slot, pg].astype(BF16)) + jnp.concatenate([r_pg] * t_new, axis=0) + qcb
        return carry

    lax.fori_loop(0, n_pages, qk, 0, unroll=PAGE_UNROLL)

    kn = _pad_rows(knew_ref[0], page).astype(BF16)
    vn = _pad_rows(vnew_ref[0], page).astype(BF16)
    lr = lrow_ref[0]
    lane8 = lax.broadcasted_iota(jnp.int32, lr.shape, 1)
    qct = jnp.zeros(lr.shape, F32)
    for t in range(t_new):
        qct = qct + jnp.where(lane8 >= t, lr[:, t:t + 1], 0.0)
    rr = lax.broadcasted_iota(jnp.int32, (rows, page), 0)
    cc = lax.broadcasted_iota(jnp.int32, (rows, page), 1)
    s_new = _dot_nt(qbd, kn) + qc - jnp.concatenate([qct] * t_new, axis=0)
    s_ref[n_pages] = jnp.where((cc * nh <= rr) & (cc < t_new), s_new, NEG)

    l = _softmax_all(s_ref, p_ref)

    def pv(pg, acc):
        return acc + _dot_nt(p_ref[pg], vbuf[slot, pg].astype(BF16))

    acc = lax.fori_loop(0, n_pages, pv, _dot(p_ref[n_pages], vn), unroll=PAGE_UNROLL)
    o = acc / l
    o_a, o_b = o[:, :LANES], o[:, LANES:]
    kv = (lax.broadcasted_iota(jnp.int32, (rows, LANES), 0) % nh) // (FOX_HEADS // FOX_KV_HEADS)
    sel = jnp.where(kv == 0, o_a, jnp.where(kv == 1, pltpu.roll(o_a, HEAD_DIM, 1),
                    jnp.where(kv == 2, o_b, pltpu.roll(o_b, HEAD_DIM, 1))))
    o_ref[0] = sel[:, :HEAD_DIM]


def _fox_sample(pt_flat, qbd, lcol, lrow, knew, vnew, ckt, cvt, clf, n_pages):
    db, rows, _ = qbd.shape
    t_new = rows // FOX_HEADS
    page = ckt.shape[-1]
    sup = jnp.asarray(np.tril(np.ones((page, page), np.float32), -1), dtype=BF16)
    per_b = lambda shape: pl.BlockSpec((1,) + shape, lambda b, pt: (b, 0, 0))
    hbm = pl.BlockSpec(memory_space=pl.ANY)
    in_specs = [per_b((rows, MXU_DIM)), per_b((rows, 1)), per_b((FOX_HEADS, LANES)),
                per_b((8, FOX_KV_WIDTH)), per_b((8, FOX_KV_WIDTH)),
                pl.BlockSpec((page, page), lambda b, pt: (0, 0)), hbm, hbm, hbm]
    return pl.pallas_call(
        functools.partial(_fox_sample_kernel, n_pages=n_pages, t_new=t_new),
        out_shape=jax.ShapeDtypeStruct((db, rows, HEAD_DIM), F32),
        grid_spec=pltpu.PrefetchScalarGridSpec(
            num_scalar_prefetch=1,
            grid=(db,),
            in_specs=in_specs,
            out_specs=pl.BlockSpec((1, rows, HEAD_DIM), lambda b, pt: (b, 0, 0)),
            scratch_shapes=[pltpu.VMEM((2, n_pages, FOX_KV_WIDTH, page), F32),
                            pltpu.VMEM((2, n_pages, FOX_KV_WIDTH, page), F32),
                            pltpu.VMEM((2, n_pages, FOX_HEADS, page), F32),
                            pltpu.VMEM((n_pages * FOX_HEADS, page), F32),
                            pltpu.VMEM((n_pages + 1, rows, page), F32),
                            pltpu.VMEM((n_pages + 1, rows, page), BF16),
                            pltpu.SemaphoreType.DMA((3, 2))]),
        compiler_params=pltpu.CompilerParams(dimension_semantics=("arbitrary",),
                                             vmem_limit_bytes=VMEM_LIMIT),
        name="fox_sample",
    )(pt_flat, qbd, lcol, lrow, knew, vnew, sup, ckt, cvt, clf)


def _diff_sample_kernel(pt_ref, qbd_ref, knew_ref, vnew_ref, dl_ref, ck_hbm, cv_hbm,
                        o_ref, kbuf, vbuf, s_ref, p_ref, sem, *, n_pages, t_new, lam_init):
    slot = _page_pipeline(pt_ref, (ck_hbm, cv_hbm), (kbuf, vbuf), sem, n_pages)
    half = t_new * DIFF_HEADS
    rows = 2 * half
    page = kbuf.shape[-1]
    qbd = qbd_ref[0]

    def qk(pg, carry):
        s_ref[pg] = _dot(qbd, kbuf[slot, pg].astype(BF16))
        return carry

    lax.fori_loop(0, n_pages, qk, 0, unroll=PAGE_UNROLL)

    kn = _pad_rows(knew_ref[0], page).astype(BF16)
    vn = _pad_rows(vnew_ref[0], page).astype(BF16)
    rr = lax.broadcasted_iota(jnp.int32, (rows, page), 0)
    cc = lax.broadcasted_iota(jnp.int32, (rows, page), 1)
    t_row = (rr % half) // DIFF_HEADS
    s_ref[n_pages] = jnp.where((cc <= t_row) & (cc < t_new), _dot_nt(qbd, kn), NEG)

    l = _softmax_all(s_ref, p_ref)

    def pv(pg, acc):
        a0, a1 = acc
        pp = p_ref[pg]
        v0 = vbuf[slot, pg, pl.ds(0, page, stride=DIFF_KV_HEADS), :].astype(BF16)
        v1 = vbuf[slot, pg, pl.ds(1, page, stride=DIFF_KV_HEADS), :].astype(BF16)
        return a0 + _dot(pp, v0), a1 + _dot(pp, v1)

    w = 2 * HEAD_DIM
    new = _dot(p_ref[n_pages], vn)
    a0, a1 = lax.fori_loop(0, n_pages, pv, (new[:, :w], new[:, w:]), unroll=PAGE_UNROLL)
    kv = (lax.broadcasted_iota(jnp.int32, (rows, w), 0) % DIFF_HEADS) // (DIFF_HEADS // DIFF_KV_HEADS)
    sel = jnp.where(kv == 0, a0, a1) / l
    o_ref[0] = sel[:half] - _diff_lambda(dl_ref, lam_init) * sel[half:]


def _diff_sample(pt_flat, qbd, knew, vnew, dlam, ckt, cv, n_pages, lam_init):
    db, rows, _ = qbd.shape
    t_new = rows // (2 * DIFF_HEADS)
    page = ckt.shape[-1]
    per_b = lambda shape: pl.BlockSpec((1,) + shape, lambda b, pt: (b, 0, 0))
    hbm = pl.BlockSpec(memory_space=pl.ANY)
    in_specs = [per_b((rows, MXU_DIM)), per_b((8, DIFF_KV_WIDTH)), per_b((8, DIFF_KV_WIDTH)),
                pl.BlockSpec((4, HEAD_DIM), lambda b, pt: (0, 0)), hbm, hbm]
    return pl.pallas_call(
        functools.partial(_diff_sample_kernel, n_pages=n_pages, t_new=t_new, lam_init=lam_init),
        out_shape=jax.ShapeDtypeStruct((db, rows // 2, 2 * HEAD_DIM), F32),
        grid_spec=pltpu.PrefetchScalarGridSpec(
            num_scalar_prefetch=1,
            grid=(db,),
            in_specs=in_specs,
            out_specs=pl.BlockSpec((1, rows // 2, 2 * HEAD_DIM), lambda b, pt: (b, 0, 0)),
            scratch_shapes=[pltpu.VMEM((2, n_pages, DIFF_KV_WIDTH, page), F32),
                            pltpu.VMEM((2, n_pages, DIFF_KV_HEADS * page, 2 * HEAD_DIM), F32),
                            pltpu.VMEM((n_pages + 1, rows, page), F32),
                            pltpu.VMEM((n_pages + 1, rows, page), BF16),
                            pltpu.SemaphoreType.DMA((2, 2))]),
        compiler_params=pltpu.CompilerParams(dimension_semantics=("arbitrary",),
                                             vmem_limit_bytes=VMEM_LIMIT),
        name="diff_sample",
    )(pt_flat, qbd, knew, vnew, dlam, ckt, cv)


def _heads_rmsnorm_t(xt, width, gain):
    parts = []
    for h in range(xt.shape[0] // width):
        xh = xt[h * width:(h + 1) * width]
        parts.append(xh * lax.rsqrt(jnp.mean(xh * xh, axis=0, keepdims=True) + EPS))
    return jnp.concatenate(parts, axis=0) * jnp.concatenate([gain] * (xt.shape[1] // LANES), axis=1)


def _ffn_kernel(*refs, tm, d_ff, post_scale, stateful):
    if stateful:
        _ffn_tile(refs, tm, d_ff, post_scale, stateful)
        return
    carry_ref, meta_ref = refs[-2:]
    b, i = pl.program_id(0), pl.program_id(1)

    @pl.when((i > 0) | (b == 0))
    def _():
        _ffn_tile(refs[:-1], tm, d_ff, post_scale, stateful)

        @pl.when(i == 0)
        def _():
            meta_ref[...] = carry_ref[...]

    @pl.when((i == 0) & (b > 0))
    def _():
        carry_ref[...] = meta_ref[...]


def _ffn_tile(refs, tm, d_ff, post_scale, stateful):
    if stateful:
        (x_ref, fo_ref, do_ref, gm64_ref, gm128_ref, gfo_ref, gdo_ref, wo_ref, gffn_ref, wup_ref,
         cw_ref, cb_ref, wdn_ref, s0_ref, s1_ref, y_ref, h_ref, hs_ref) = refs
        fo = _group_rmsnorm(fo_ref[0], gm64_ref[...], gfo_ref[...])
        do = _group_rmsnorm(do_ref[0], gm128_ref[...], gdo_ref[...]) * post_scale
        merged = jnp.concatenate([fo, do], axis=1)
    else:
        (x_ref, head_ref, fo_ref, do_ref, gfo_ref, gdo_ref, wo_ref, gffn_ref, wup_ref,
         cw_ref, cb_ref, wdn_ref, y_ref, h_ref, hs_ref, carry_ref) = refs

        @pl.when(pl.program_id(1) == 0)
        def _():
            carry_ref[...] = jnp.zeros_like(carry_ref)

        fo = _heads_rmsnorm_t(fo_ref[0], HEAD_DIM, gfo_ref[...])
        do = _heads_rmsnorm_t(do_ref[0], 2 * HEAD_DIM, gdo_ref[...]) * post_scale
        merged = jnp.concatenate([fo, do], axis=0).T

    att = _dot(merged.astype(BF16), wo_ref[...])
    x = x_ref[0]
    if not stateful:
        x = jnp.where(pl.program_id(1) == 0, head_ref[...], x)
    x1 = x + att
    hn = x1 * lax.rsqrt(jnp.mean(x1 * x1, axis=-1, keepdims=True) + EPS) * gffn_ref[...]
    hnb = hn.astype(BF16)
    ch = FF_CHUNK
    n_chunks = d_ff // ch
    rid = lax.broadcasted_iota(jnp.int32, (8, ch), 0)

    def produce(c):
        for half in range(2):
            col = half * d_ff + c * ch
            hs_ref[c % 2, half] = _dot(hnb, wup_ref[:, col:col + ch])

    def consume(c, acc):
        conv = []
        for half in range(2):
            col = half * d_ff + c * ch
            h = hs_ref[c % 2, half]
            if stateful:
                nseq = s0_ref.shape[0]
                hm1 = jnp.concatenate([s1_ref[:, col:col + ch], h[:tm - nseq]], axis=0)
                hm2 = jnp.concatenate([s0_ref[:, col:col + ch], s1_ref[:, col:col + ch], h[:tm - 2 * nseq]], axis=0)
                h_ref[:, col:col + ch] = h[tm - 2 * nseq:]
            else:
                prev = carry_ref[:, col:col + ch]
                r1, r2 = pltpu.roll(h, 1, 0), pltpu.roll(h, 2, 0)
                top1 = jnp.where(rid < 1, pltpu.roll(prev, 1, 0), r1[0:8])
                top2 = jnp.where(rid < 2, pltpu.roll(prev, 2, 0), r2[0:8])
                hm1 = jnp.concatenate([top1, r1[8:]], axis=0)
                hm2 = jnp.concatenate([top2, r2[8:]], axis=0)
                carry_ref[:, col:col + ch] = h[tm - 8:]
                h_ref[0, :, col:col + ch] = h[tm - 8:]
            conv.append(cb_ref[:, col:col + ch] + cw_ref[0:1, col:col + ch] * hm2
                        + cw_ref[1:2, col:col + ch] * hm1 + cw_ref[2:3, col:col + ch] * h)
        g, u = conv
        act = g * (1.0 / (1.0 + jnp.exp(-g))) * u
        return acc + _dot(act.astype(BF16), wdn_ref[c * ch:(c + 1) * ch, :])

    acc = x1
    produce(0)
    for c in range(n_chunks):
        if c + 1 < n_chunks:
            produce(c + 1)
        acc = consume(c, acc)
    y_ref[0] = acc


def _merge_ffn(x, fo, do, p, tm, state=None, head=None):
    nb, lt, d = x.shape
    d_ff = p["w_down"].shape[0]
    stateful = state is not None
    single = pl.Buffered(1)
    const = lambda shape: pl.BlockSpec(shape, lambda b, i: (0,) * len(shape), pipeline_mode=single)
    weights = [const((FOX_WIDTH + DIFF_WIDTH, d)), const((1, d)), const((d, 2 * d_ff)), const((CONV_W, 2 * d_ff)),
               const((1, 2 * d_ff)), const((d_ff, d))]
    wargs = [p["w_o"], p["g_ffn"], p["w_up"], p["conv_w"], p["conv_b"], p["w_down"]]
    if stateful:
        nseq = state[0].shape[0]
        row = lambda w: pl.BlockSpec((1, tm, w), lambda b, i: (b, i, 0), pipeline_mode=single)
        in_specs = ([row(d), row(FOX_WIDTH), row(DIFF_WIDTH), const((MXU_DIM, MXU_DIM)), const((MXU_DIM, MXU_DIM)),
                     const((1, FOX_WIDTH)), const((1, DIFF_WIDTH))] + weights
                    + [const((nseq, 2 * d_ff)), const((nseq, 2 * d_ff))])
        args = [x, fo, do, p["gm64"], p["gm128"], p["g_fo"], p["g_do"]] + wargs + list(state)
        h_shape = jax.ShapeDtypeStruct((2 * nseq, 2 * d_ff), F32)
        h_spec = pl.BlockSpec((2 * nseq, 2 * d_ff), lambda b, i: (0, 0))
        scratch = [pltpu.VMEM((2, 2, tm, FF_CHUNK), F32)]
    else:
        real = pl.BlockSpec((1, tm, d), lambda b, i: (b, jnp.maximum(i - 1, 0), 0))
        in_specs = ([real, const((tm, d)),
                     pl.BlockSpec((1, FOX_WIDTH, tm), lambda b, i: (b, 0, i)),
                     pl.BlockSpec((1, DIFF_WIDTH, tm), lambda b, i: (b, 0, i)),
                     const((FOX_WIDTH, LANES)), const((DIFF_WIDTH, LANES))] + weights)
        args = [x, head, fo, do, p["g_fo_t"], p["g_do_t"]] + wargs
        h_shape = jax.ShapeDtypeStruct((nb, 8, 2 * d_ff), F32)
        h_spec = pl.BlockSpec((1, 8, 2 * d_ff), lambda b, i: (b, 0, 0))
        scratch = [pltpu.VMEM((2, 2, tm, FF_CHUNK), F32), pltpu.VMEM((8, 2 * d_ff), F32),
                   pltpu.VMEM((8, 2 * d_ff), F32)]
    y_spec = pl.BlockSpec((1, tm, d), lambda b, i: (b, i, 0)) if stateful else real
    n_tiles = lt // tm + (0 if stateful else 1)
    kern = functools.partial(_ffn_kernel, tm=tm, d_ff=d_ff, post_scale=p["post_scale"], stateful=stateful)
    return pl.pallas_call(
        kern,
        out_shape=[jax.ShapeDtypeStruct((nb, lt, d), F32), h_shape],
        grid=(nb, n_tiles),
        in_specs=in_specs,
        out_specs=[y_spec, h_spec],
        scratch_shapes=scratch,
        compiler_params=pltpu.CompilerParams(dimension_semantics=("arbitrary", "arbitrary"),
                                             vmem_limit_bytes=VMEM_LIMIT),
        name="merge_ffn_sample" if stateful else "merge_ffn_prompt",
    )(*args)


def kernel(x_prompt, x_sample, cache_fox_k, cache_fox_v, cache_fox_logf, cache_diff_k, cache_diff_v, state_ffn_conv, page_table, meta_tokens, attn_norm_g, w_in, b_f, fox_qn_g, fox_kn_g, fox_on_g, diff_qn_g, diff_kn_g, diff_lambda, diff_subln_g, w_o, ffn_norm_g, w_up, conv_w, conv_b, w_down):
    nb, seq, d = x_prompt.shape
    db, t_new, _ = x_sample.shape
    depth, n_phys, page = cache_fox_k.shape[:3]
    n_pages = page_table.shape[1]
    assert depth == 1, "single layer only"
    layer = 0
    lam_init = 0.8 - 0.6 * math.exp(-0.3 * layer)
    past_len = n_pages * page

    wi = w_in[layer]
    o = np.cumsum([0, FOX_WIDTH, FOX_KV_WIDTH, FOX_KV_WIDTH, FOX_HEADS, DIFF_WIDTH, DIFF_KV_WIDTH, DIFF_KV_WIDTH])
    seg = lambda k: wi[:, o[k]:o[k + 1]]
    w_all = jnp.concatenate([seg(0), seg(1), seg(2), seg(4), seg(5), seg(6),
                             jnp.pad(seg(3), ((0, 0), (0, LANES - FOX_HEADS)))], axis=1).astype(BF16)
    g_fo = fox_on_g[layer].reshape(1, FOX_WIDTH)
    g_do = jnp.tile(diff_subln_g[layer], DIFF_HEADS)[None, :]
    lane_rep = lambda g: jnp.broadcast_to(g.reshape(-1, 1), (g.shape[-1], LANES))
    p = {
        "g_attn": attn_norm_g[layer][None, :],
        "w_in": w_all,
        "gm64": _block_diag_mean(HEAD_DIM),
        "gm128": _block_diag_mean(2 * HEAD_DIM),
        "g_heads": jnp.concatenate([jnp.tile(fox_qn_g[layer], FOX_HEADS), jnp.tile(fox_kn_g[layer], FOX_KV_HEADS),
                                    jnp.tile(diff_qn_g[layer], 2 * DIFF_HEADS),
                                    jnp.tile(diff_kn_g[layer], 2 * DIFF_KV_HEADS)])[None, :],
        "b_f": jnp.pad(b_f[layer], (0, LANES - FOX_HEADS))[None, :],
        "g_fo": g_fo, "g_do": g_do, "g_fo_t": lane_rep(g_fo), "g_do_t": lane_rep(g_do),
        "post_scale": 1.0 - lam_init,
        "w_o": w_o[layer].astype(BF16),
        "g_ffn": ffn_norm_g[layer][None, :],
        "w_up": w_up[layer].astype(BF16),
        "conv_w": conv_w[layer],
        "conv_b": conv_b[layer][None, :],
        "w_down": w_down[layer].astype(BF16),
    }
    dlam = diff_lambda[layer]

    tm = ROW_TILE
    pad = (-N_META) % tm
    lp = pad + N_META + seq
    assert pad + N_META == tm and seq % tm == 0
    head = jnp.concatenate([jnp.zeros((pad, d), x_prompt.dtype), meta_tokens.astype(x_prompt.dtype)], axis=0)
    pos_p = jnp.arange(lp) - pad
    fk, fv, lf, dk, dv, qat, ka, vt, dqt, dka, dvt = _project(x_prompt, pos_p, tm, p, head=head)
    fo_t = _fox_prompt(qat, ka, vt, tm, pad)
    do_t = _diff_prompt(dqt, dka, dvt, dlam, tm, pad, lam_init)
    y_prompt, hlast = _merge_ffn(x_prompt, fo_t, do_t, p, tm, head=head)
    ltot = N_META + seq
    pk = fk[:, pad:].reshape(1, nb, ltot, FOX_KV_HEADS, HEAD_DIM)
    pv = fv[:, pad:].reshape(1, nb, ltot, FOX_KV_HEADS, HEAD_DIM)
    pf = lf[:, pad:][None]
    pdk = dk[:, pad:].reshape(1, nb, ltot, DIFF_KV_HEADS, 2, HEAD_DIM)
    pdv = dv[:, pad:].reshape(1, nb, ltot, DIFF_KV_HEADS, 2 * HEAD_DIM)
    pc = hlast[:, 8 - (CONV_W - 1):][None]

    rows = db * t_new
    pos_s = past_len + (jnp.arange(rows) % t_new)
    qf, fk, fv, lf, dqb, dk, dv = _project(x_sample.reshape(1, rows, d), pos_s, min(ROW_TILE, rows), p)
    g_f = FOX_HEADS // FOX_KV_HEADS
    g_d = DIFF_HEADS // DIFF_KV_HEADS
    eye_f = jnp.eye(FOX_KV_HEADS, dtype=BF16)
    qbd_f = jnp.einsum('btkgd,kj->btkgjd', qf.reshape(db, t_new, FOX_KV_HEADS, g_f, HEAD_DIM), eye_f)
    qbd_f = qbd_f.reshape(db, t_new * FOX_HEADS, FOX_KV_WIDTH)
    eye_d, eye_2 = jnp.eye(DIFF_KV_HEADS, dtype=BF16), jnp.eye(2, dtype=BF16)
    qbd_d = jnp.einsum('btkgwd,kj,wv->bwtkgjvd', dqb.reshape(db, t_new, DIFF_KV_HEADS, g_d, 2, HEAD_DIM), eye_d, eye_2)
    qbd_d = qbd_d.reshape(db, 2 * t_new * DIFF_HEADS, DIFF_KV_WIDTH)
    lf_s = lf.reshape(db, t_new, FOX_HEADS)
    lcol = lf_s.reshape(db, t_new * FOX_HEADS, 1)
    lrow = jnp.pad(jnp.swapaxes(lf_s, 1, 2), ((0, 0), (0, 0), (0, LANES - t_new)))
    new_rows = lambda a: jnp.pad(a.reshape(db, t_new, a.shape[-1]), ((0, 0), (0, 8 - t_new), (0, 0)))
    pt_flat = page_table.reshape(-1).astype(jnp.int32)
    ckt = jnp.transpose(cache_fox_k[layer], (0, 2, 3, 1)).reshape(n_phys, FOX_KV_WIDTH, page)
    cvt = jnp.transpose(cache_fox_v[layer], (0, 2, 3, 1)).reshape(n_phys, FOX_KV_WIDTH, page)
    clf = jnp.swapaxes(cache_fox_logf[layer], 1, 2)
    fo_s = _fox_sample(pt_flat, qbd_f, lcol, lrow, new_rows(fk), new_rows(fv), ckt, cvt, clf, n_pages)
    cdkt = jnp.transpose(cache_diff_k[layer], (0, 2, 3, 4, 1)).reshape(n_phys, DIFF_KV_WIDTH, page)
    cdv = cache_diff_v[layer].reshape(n_phys, page * DIFF_KV_HEADS, 2 * HEAD_DIM)
    do_s = _diff_sample(pt_flat, qbd_d, new_rows(dk), new_rows(dv), dlam, cdkt, cdv, n_pages, lam_init)

    tmajor = lambda a, w: jnp.swapaxes(a.reshape(db, t_new, w), 0, 1).reshape(1, rows, w)
    st = state_ffn_conv[layer]
    ys, hs = _merge_ffn(tmajor(x_sample, d), tmajor(fo_s, FOX_WIDTH), tmajor(do_s, DIFF_WIDTH), p, rows,
                        state=(st[:, 0], st[:, 1]))
    y_sample = jnp.swapaxes(ys.reshape(t_new, db, d), 0, 1)
    sc = jnp.swapaxes(hs.reshape(CONV_W - 1, db, hs.shape[-1]), 0, 1)[None]
    sk = fk.reshape(1, db, t_new, FOX_KV_HEADS, HEAD_DIM)
    sv = fv.reshape(1, db, t_new, FOX_KV_HEADS, HEAD_DIM)
    sf = lf_s[None]
    sdk = dk.reshape(1, db, t_new, DIFF_KV_HEADS, 2, HEAD_DIM)
    sdv = dv.reshape(1, db, t_new, DIFF_KV_HEADS, 2 * HEAD_DIM)
    return (y_prompt, y_sample, pk, pv, pf, pdk, pdv, pc, sk, sv, sf, sdk, sdv, sc)
```
